```python
import math, functools
import jax, jax.numpy as jnp
from jax import lax
import numpy as np


D_MODEL = 1024
BATCH = 4
SEQ = 8192
DEPTH = 1
DEC_BATCH = 128
DEC_SEQ = 1
PAST_LEN = 8192
PAGE_SIZE = 128

HEAD_DIM = 64
N_HEADS = D_MODEL // (2 * HEAD_DIM)
ATTN_W = 2 * N_HEADS * HEAD_DIM
D_CONV = D_MODEL
CONV_W = 3
NUM_BUCKETS = 32
MAX_DISTANCE = 128
Q_BLOCK = 128
PEER_HEADS = 8
N_KEYS = 128
N_EXPERTS = N_KEYS * N_KEYS
D_KEY = 128
D_KEY_HALF = D_KEY // 2
PEER_TOPK = 16
PEER_BLOCK = 128
D_PLE = 256
ALPHA = (2.0 * DEPTH) ** 0.25
BETA = (8.0 * DEPTH) ** -0.25
LN_EPS = 1e-5
RMS_EPS = 1e-5
N_PAGES = PAST_LEN // PAGE_SIZE

kernel_name = 'hybrid_conv_diffattn_peer_step'


def layer_norm(x, g, b):
    xf = x.astype(jnp.float32)
    mu = jnp.mean(xf, axis=-1, keepdims=True)
    xc = xf - mu
    var = jnp.mean(xc * xc, axis=-1, keepdims=True)
    return (xc * lax.rsqrt(var + LN_EPS) * g.astype(jnp.float32) + b.astype(jnp.float32)).astype(x.dtype)


def t5_bucket(qpos, kpos):
    n = jnp.maximum(qpos[:, None] - kpos[None, :], 0)
    max_exact = NUM_BUCKETS // 2
    nf = jnp.maximum(n, 1).astype(jnp.float32)
    large = max_exact + (jnp.log(nf / max_exact) / math.log(MAX_DISTANCE / max_exact)
                         * (NUM_BUCKETS - max_exact)).astype(jnp.int32)
    large = jnp.minimum(large, NUM_BUCKETS - 1)
    return jnp.where(n < max_exact, n, large)


def diff_softmax_attend(q, k, v, qpos, kpos, lam, rel_bias):
    s = jnp.einsum('bqhcd,bkhcd->bhcqk', q.astype(jnp.float32), k.astype(jnp.float32)) * (HEAD_DIM ** -0.5)
    bias = jnp.transpose(rel_bias.astype(jnp.float32)[t5_bucket(qpos, kpos)], (2, 0, 1))
    causal = kpos[None, :] <= qpos[:, None]
    s = jnp.where(causal, s + bias[None, :, None], -jnp.inf)
    a = jax.nn.softmax(s, axis=-1)
    a = a[:, :, 0] - lam * a[:, :, 1]
    return jnp.einsum('bhqk,bkhe->bqhe', a, v.astype(jnp.float32))


def attend_prompt(q, k, v, lam, rel_bias):
    bsz, seq = q.shape[0], q.shape[1]
    n_blocks = seq // Q_BLOCK
    kpos = jnp.arange(seq, dtype=jnp.int32)

    def block(b):
        start = b * Q_BLOCK
        qb = lax.dynamic_slice_in_dim(q, start, Q_BLOCK, axis=1)
        qpos = start + jnp.arange(Q_BLOCK, dtype=jnp.int32)
        return diff_softmax_attend(qb, k, v, qpos, kpos, lam, rel_bias)

    o = lax.map(block, jnp.arange(n_blocks, dtype=jnp.int32))
    return jnp.transpose(o, (1, 0, 2, 3, 4)).reshape(bsz, seq, N_HEADS, 2 * HEAD_DIM)


def attend_sample(q, k, v, lam, cache_k, cache_v, layer, page_table, rel_bias):
    n_pages = page_table.shape[1]
    past = n_pages * cache_k.shape[2]
    t_new = q.shape[1]
    qpos = past + jnp.arange(t_new, dtype=jnp.int32)
    kpos = jnp.arange(past + t_new, dtype=jnp.int32)

    def one(args):
        qs, ks, vs, pt = args
        k_all = jnp.concatenate(
            [cache_k[layer, pt].reshape(past, N_HEADS, 2, HEAD_DIM).astype(ks.dtype), ks], axis=0)
        v_all = jnp.concatenate(
            [cache_v[layer, pt].reshape(past, N_HEADS, 2 * HEAD_DIM).astype(vs.dtype), vs], axis=0)
        return diff_softmax_attend(qs[None], k_all[None], v_all[None], qpos, kpos, lam, rel_bias)[0]

    return lax.map(one, (q, k, v, page_table))


def peer(x, wq, keys, u, v):
    shp = x.shape
    xf = x.reshape(-1, shp[-1])
    n = xf.shape[0]
    nb = -(-n // PEER_BLOCK)
    xf = jnp.pad(xf, ((0, nb * PEER_BLOCK - n), (0, 0)))
    xb = xf.reshape(nb, PEER_BLOCK, shp[-1])

    def block(xt):
        q = jnp.einsum('td,de->te', xt, wq).reshape(PEER_BLOCK, PEER_HEADS, 2, D_KEY_HALF)
        s = jnp.einsum('thcd,hckd->thck', q.astype(jnp.float32), keys.astype(jnp.float32))
        s_top, i_top = lax.top_k(s, PEER_TOPK)
        cand = s_top[:, :, 0, :, None] + s_top[:, :, 1, None, :]
        cidx = i_top[:, :, 0, :, None] * N_KEYS + i_top[:, :, 1, None, :]
        cand = cand.reshape(PEER_BLOCK, PEER_HEADS, PEER_TOPK * PEER_TOPK)
        cidx = cidx.reshape(PEER_BLOCK, PEER_HEADS, PEER_TOPK * PEER_TOPK)
        best, pos = lax.top_k(cand, PEER_TOPK)
        eidx = jnp.take_along_axis(cidx, pos, axis=-1)
        g = jax.nn.softmax(best, axis=-1)
        ue = u[eidx].astype(jnp.float32)
        act = jax.nn.gelu(jnp.einsum('td,thkd->thk', xt.astype(jnp.float32), ue))
        ve = v[eidx].astype(jnp.float32)
        return jnp.einsum('thk,thkd->td', g * act, ve)

    y = lax.map(block, xb).reshape(nb * PEER_BLOCK, shp[-1])[:n]
    return y.reshape(shp).astype(x.dtype)


def run_layer(x, p, conv_prev, attend, lam, lam_init, w_in, conv_w, w_conv_out, w_attn_out, w_o,
              subln_g, ln1_g, ln1_b, ln2_g, ln2_b, peer_wq, peer_keys, peer_u, peer_v,
              ple_proj, ple_gate):
    bsz, t = x.shape[0], x.shape[1]
    z = jnp.einsum('btd,de->bte', x, w_in)
    widths = (D_CONV, D_CONV, D_CONV, ATTN_W, ATTN_W, ATTN_W, D_MODEL, D_MODEL)
    offs = np.cumsum(widths)[:-1].tolist()
    h, b_gate, c_gate, q, k, v, g_conv, g_attn = jnp.split(z, offs, axis=-1)

    u = c_gate * h
    u_pad = jnp.concatenate([conv_prev.astype(u.dtype), u], axis=1)
    conv = u_pad[:, 0:t] * conv_w[0]
    for j in range(1, CONV_W):
        conv = conv + u_pad[:, j:j + t] * conv_w[j]
    y_conv = jnp.einsum('btc,cd->btd', b_gate * conv, w_conv_out)
    new_conv = u_pad[:, t:]

    q = q.reshape(bsz, t, N_HEADS, 2, HEAD_DIM)
    k = k.reshape(bsz, t, N_HEADS, 2, HEAD_DIM)
    v = v.reshape(bsz, t, N_HEADS, 2 * HEAD_DIM)
    o = attend(q, k, v, lam)
    o = o * lax.rsqrt(jnp.mean(o * o, axis=-1, keepdims=True) + RMS_EPS) \
        * subln_g.astype(jnp.float32) * (1.0 - lam_init)
    y_attn = jnp.einsum('bte,ed->btd', o.reshape(bsz, t, ATTN_W).astype(x.dtype), w_attn_out)

    merged = jax.nn.sigmoid(g_conv) * y_conv + jax.nn.sigmoid(g_attn) * y_attn
    mix = jnp.einsum('btd,de->bte', merged, w_o)
    x1 = layer_norm(ALPHA * x + mix, ln1_g, ln1_b)
    x2 = layer_norm(ALPHA * x1 + peer(x1, peer_wq, peer_keys, peer_u, peer_v), ln2_g, ln2_b)

    gate = jax.nn.sigmoid(jnp.einsum('btd,de->bte', x2, ple_gate))
    y = x2 + gate * jnp.einsum('btp,pd->btd', p.astype(x.dtype), ple_proj)
    return y, k, v, new_conv


def setup_inputs(seed: int = 0) -> dict:
    key = jax.random.key(seed)
    ks = jax.random.split(key, 32)
    f32 = jnp.float32

    def nrm(k, shape, scale):
        return jax.random.normal(k, shape, f32) * scale

    n_used = DEC_BATCH * N_PAGES
    n_phys = n_used + max(1, n_used // 4)
    d_in = 3 * D_CONV + 3 * ATTN_W + 2 * D_MODEL
    page_table = jax.random.permutation(ks[7], n_phys)[:n_used].reshape(DEC_BATCH, N_PAGES).astype(jnp.int32)
    return {
        'x_prompt': nrm(ks[0], (BATCH, SEQ, D_MODEL), 1.0),
        'x_sample': nrm(ks[1], (DEC_BATCH, DEC_SEQ, D_MODEL), 1.0),
        'p_prompt': nrm(ks[2], (DEPTH, BATCH, SEQ, D_PLE), 1.0),
        'p_sample': nrm(ks[3], (DEPTH, DEC_BATCH, DEC_SEQ, D_PLE), 1.0),
        'cache_k': nrm(ks[4], (DEPTH, n_phys, PAGE_SIZE, N_HEADS, 2, HEAD_DIM), 1.0),
        'cache_v': nrm(ks[5], (DEPTH, n_phys, PAGE_SIZE, N_HEADS, 2 * HEAD_DIM), 1.0),
        'state_conv': nrm(ks[6], (DEPTH, DEC_BATCH, CONV_W - 1, D_CONV), 1.0),
        'page_table': page_table,
        'rel_bias': nrm(ks[8], (NUM_BUCKETS, N_HEADS), 0.5),
        'w_in': nrm(ks[9], (DEPTH, D_MODEL, d_in), D_MODEL ** -0.5),
        'conv_w': nrm(ks[10], (DEPTH, CONV_W, D_CONV), CONV_W ** -0.5),
        'w_conv_out': nrm(ks[11], (DEPTH, D_CONV, D_MODEL), D_CONV ** -0.5),
        'w_attn_out': nrm(ks[12], (DEPTH, ATTN_W, D_MODEL), ATTN_W ** -0.5),
        'w_o': nrm(ks[13], (DEPTH, D_MODEL, D_MODEL), BETA * D_MODEL ** -0.5),
        'lambda_q1': nrm(ks[14], (DEPTH, HEAD_DIM), 0.1),
        'lambda_k1': nrm(ks[15], (DEPTH, HEAD_DIM), 0.1),
        'lambda_q2': nrm(ks[16], (DEPTH, HEAD_DIM), 0.1),
        'lambda_k2': nrm(ks[17], (DEPTH, HEAD_DIM), 0.1),
        'subln_g': 1.0 + nrm(ks[18], (DEPTH, 2 * HEAD_DIM), 0.01),
        'ln1_g': 1.0 + nrm(ks[19], (DEPTH, D_MODEL), 0.01),
        'ln1_b': nrm(ks[20], (DEPTH, D_MODEL), 0.01),
        'ln2_g': 1.0 + nrm(ks[21], (DEPTH, D_MODEL), 0.01),
        'ln2_b': nrm(ks[22], (DEPTH, D_MODEL), 0.01),
        'peer_wq': nrm(ks[23], (DEPTH, D_MODEL, PEER_HEADS * D_KEY), D_MODEL ** -0.5),
        'peer_keys': nrm(ks[24], (DEPTH, PEER_HEADS, 2, N_KEYS, D_KEY_HALF), D_KEY_HALF ** -0.5),
        'peer_u': nrm(ks[25], (DEPTH, N_EXPERTS, D_MODEL), D_MODEL ** -0.5),
        'peer_v': nrm(ks[26], (DEPTH, N_EXPERTS, D_MODEL), BETA),
        'ple_proj': nrm(ks[27], (DEPTH, D_PLE, D_MODEL), D_PLE ** -0.5),
        'ple_gate': nrm(ks[28], (DEPTH, D_MODEL, D_MODEL), D_MODEL ** -0.5),
    }


def reference(x_prompt, x_sample, p_prompt, p_sample, cache_k, cache_v, state_conv, page_table,
              rel_bias, w_in, conv_w, w_conv_out, w_attn_out, w_o, lambda_q1, lambda_k1,
              lambda_q2, lambda_k2, subln_g, ln1_g, ln1_b, ln2_g, ln2_b, peer_wq, peer_keys,
              peer_u, peer_v, ple_proj, ple_gate):
    y_p, y_s = x_prompt, x_sample
    kp_l, vp_l, cp_l, ks_l, vs_l, cs_l = [], [], [], [], [], []
    conv_zero = jnp.zeros((x_prompt.shape[0], CONV_W - 1, D_CONV), x_prompt.dtype)
    for i in range(DEPTH):
        lam_init = 0.8 - 0.6 * math.exp(-0.3 * i)
        lam = (jnp.exp(jnp.sum(lambda_q1[i].astype(jnp.float32) * lambda_k1[i].astype(jnp.float32)))
               - jnp.exp(jnp.sum(lambda_q2[i].astype(jnp.float32) * lambda_k2[i].astype(jnp.float32)))
               + lam_init)
        weights = (w_in[i], conv_w[i], w_conv_out[i], w_attn_out[i], w_o[i], subln_g[i],
                   ln1_g[i], ln1_b[i], ln2_g[i], ln2_b[i], peer_wq[i], peer_keys[i],
                   peer_u[i], peer_v[i], ple_proj[i], ple_gate[i])
        attend_p = functools.partial(attend_prompt, rel_bias=rel_bias)
        attend_s = functools.partial(attend_sample, cache_k=cache_k, cache_v=cache_v, layer=i,
                                     page_table=page_table, rel_bias=rel_bias)
        y_p, k_p, v_p, c_p = run_layer(y_p, p_prompt[i], conv_zero, attend_p, lam, lam_init, *weights)
        y_s, k_s, v_s, c_s = run_layer(y_s, p_sample[i], state_conv[i], attend_s, lam, lam_init, *weights)
        kp_l.append(k_p); vp_l.append(v_p); cp_l.append(c_p)
        ks_l.append(k_s); vs_l.append(v_s); cs_l.append(c_s)
    k_prompt = jnp.stack(kp_l)
    v_prompt = jnp.stack(vp_l)
    conv_prompt = jnp.stack(cp_l)
    k_sample = jnp.stack(ks_l)
    v_sample = jnp.stack(vs_l)
    conv_sample = jnp.stack(cs_l)
    return (y_p, y_s, k_prompt, v_prompt, conv_prompt, k_sample, v_sample, conv_sample)
```

```python
import functools
import math

import jax
import jax.numpy as jnp
from jax import lax
from jax.experimental import pallas as pl
from jax.experimental.pallas import tpu as pltpu

F32 = jnp.float32
BF16 = jnp.bfloat16

HEAD_DIM = 64
MAX_DISTANCE = 128
PEER_TOPK = 16
LN_EPS = 1e-5
RMS_EPS = 1e-5
NEG_BIG = -1e30
LANE = 128
SUBLANE = 8
VMEM_LIMIT_BYTES = 56 * 1024 * 1024


def _cparams(n_axes):
    return pltpu.CompilerParams(dimension_semantics=("arbitrary",) * n_axes,
                                vmem_limit_bytes=VMEM_LIMIT_BYTES)


def _resident(shape):
    zeros = (0,) * len(shape)
    return pl.BlockSpec(shape, lambda *_: zeros, pipeline_mode=pl.Buffered(1))


def _sigmoid(x):
    return 1.0 / (1.0 + jnp.exp(-x))


def _layer_norm(x, g, b):
    mu = jnp.mean(x, axis=-1, keepdims=True)
    xc = x - mu
    var = jnp.mean(xc * xc, axis=-1, keepdims=True)
    return xc * lax.rsqrt(var + LN_EPS) * g + b


def _lambda(lq1, lk1, lq2, lk2, lam_init):
    s1 = jnp.sum(lq1 * lk1, axis=-1, keepdims=True)
    s2 = jnp.sum(lq2 * lk2, axis=-1, keepdims=True)
    return jnp.exp(s1) - jnp.exp(s2) + lam_init


def _inproj_body(*refs, d, tb, conv_w_len, shifted_inputs, q_scale):
    if shifted_inputs:
        (x_ref, w_ref, cw_ref, wco_ref, um2_ref, um1_ref,
         k_ref, v_ref, qb_ref, kb_ref, vb_ref, yc_ref, ga_ref, u_ref) = refs
    else:
        (x_ref, w_ref, cw_ref, wco_ref,
         k_ref, v_ref, qb_ref, kb_ref, vb_ref, yc_ref, ga_ref, nc_ref, ubuf_ref) = refs
    assert conv_w_len == 3
    xb = x_ref[0].astype(BF16)

    def proj(g):
        return jnp.dot(xb, w_ref[:, g * d:(g + 1) * d], preferred_element_type=F32)

    u = proj(2) * proj(0)
    if shifted_inputs:
        um2 = um2_ref[0]
        um1 = um1_ref[0]
        u_ref[0] = u
    else:
        @pl.when(pl.program_id(1) == 0)
        def _():
            ubuf_ref[0:SUBLANE, :] = jnp.zeros((SUBLANE, d), F32)
        ubuf_ref[SUBLANE:SUBLANE + tb, :] = u
        um1 = ubuf_ref[SUBLANE - 1:SUBLANE - 1 + tb, :]
        um2 = ubuf_ref[SUBLANE - 2:SUBLANE - 2 + tb, :]
        ubuf_ref[0:SUBLANE, :] = u[tb - SUBLANE:tb, :]
        nc_ref[0] = u[tb - 2:tb, :]
    conv = um2 * cw_ref[0:1, :] + um1 * cw_ref[1:2, :] + u * cw_ref[2:3, :]
    y_conv = jnp.dot((proj(1) * conv).astype(BF16), wco_ref[...], preferred_element_type=F32)
    yc_ref[0] = _sigmoid(proj(6)) * y_conv
    ga_ref[0] = proj(7)
    qb_ref[0] = (proj(3) * q_scale).astype(BF16)
    k = proj(4)
    k_ref[0] = k
    kb_ref[0] = k.astype(BF16)
    v = proj(5)
    v_ref[0] = v
    vb_ref[0] = v.astype(BF16)


def _inproj_call(x, w_in_b, conv_w, w_conv_out_b, state=None, *, tb):
    bsz, t, d = x.shape
    assert w_in_b.shape == (d, 8 * d)
    q_scale = HEAD_DIM ** -0.5
    shifted = state is not None
    if shifted:
        assert t == 1
        xr = x.reshape(1, bsz, d)
        rows, nb, nt = bsz, 1, 1
        tb = bsz
    else:
        xr = x
        rows, nb, nt = t, bsz, t // tb
        assert t % tb == 0 and tb % SUBLANE == 0
    blk = lambda: pl.BlockSpec((1, tb, d), lambda b, i: (b, i, 0))
    in_specs = [blk(), _resident((d, 8 * d)), _resident(conv_w.shape), _resident((d, d))]
    args = [xr, w_in_b, conv_w, w_conv_out_b]
    f32_out = jax.ShapeDtypeStruct((nb, rows, d), F32)
    bf_out = jax.ShapeDtypeStruct((nb, rows, d), BF16)
    out_shape = [f32_out, f32_out, bf_out, bf_out, bf_out, f32_out, f32_out]
    out_specs = [blk() for _ in range(7)]
    scratch = []
    if shifted:
        in_specs += [blk(), blk()]
        args += [state[:, 0, :].reshape(1, bsz, d), state[:, 1, :].reshape(1, bsz, d)]
        out_shape.append(f32_out)
        out_specs.append(blk())
    else:
        out_shape.append(jax.ShapeDtypeStruct((nb, 2, d), F32))
        out_specs.append(pl.BlockSpec((1, 2, d), lambda b, i: (b, 0, 0)))
        scratch.append(pltpu.VMEM((tb + SUBLANE, d), F32))
    body = functools.partial(_inproj_body, d=d, tb=tb, conv_w_len=conv_w.shape[0],
                             shifted_inputs=shifted, q_scale=q_scale)
    return pl.pallas_call(
        body, grid=(nb, nt), in_specs=in_specs, out_specs=out_specs, out_shape=out_shape,
        scratch_shapes=scratch, compiler_params=_cparams(2), name="inproj_conv")(*args)


def _t5_bucket(n, num_buckets):
    n = jnp.maximum(n, 0)
    max_exact = num_buckets // 2
    nf = jnp.maximum(n, 1).astype(F32)
    large = max_exact + (jnp.log(nf / max_exact) / math.log(MAX_DISTANCE / max_exact)
                         * (num_buckets - max_exact)).astype(jnp.int32)
    large = jnp.minimum(large, num_buckets - 1)
    return jnp.where(n < max_exact, n, large)


def _bias_by_distance(rel_bias, dist):
    nbk = rel_bias.shape[0]
    rb = rel_bias.astype(F32) - rel_bias.astype(F32)[nbk - 1][None, :]
    return jnp.moveaxis(rb[_t5_bucket(dist, nbk)], -1, 0)


def _attn_prompt_body(q_ref, k_ref, v_ref, bias_ref, g_ref, lq1_ref, lk1_ref, lq2_ref, lk2_ref,
                      o_ref, m_ref, l_ref, acc_ref, *, tq, lam_init):
    qi = pl.program_id(2)
    q = q_ref[0]
    lane = lax.broadcasted_iota(jnp.int32, q.shape, 1)
    zero = jnp.zeros_like(q)
    qq = jnp.concatenate([jnp.where(lane < HEAD_DIM, q, zero),
                          jnp.where(lane >= HEAD_DIM, q, zero)], axis=0)

    def scores(ki):
        start = pl.multiple_of(ki * tq, tq)
        kc = k_ref[0, pl.ds(start, tq), :]
        vc = v_ref[0, pl.ds(start, tq), :]
        s = lax.dot_general(qq, kc, (((1,), (1,)), ((), ())), preferred_element_type=F32)
        return s, vc

    def update(s, vc):
        m_prev = m_ref[...]
        m_new = jnp.maximum(m_prev, jnp.max(s, axis=-1, keepdims=True))
        alpha = jnp.exp(m_prev - m_new)
        p = jnp.exp(s - m_new)
        l_ref[...] = alpha * l_ref[...] + jnp.sum(p, axis=-1, keepdims=True)
        acc_ref[...] = alpha * acc_ref[...] + jnp.dot(p.astype(BF16), vc,
                                                      preferred_element_type=F32)
        m_ref[...] = m_new

    s, vc = scores(qi)
    b0 = bias_ref[0, 0]
    s = s + jnp.concatenate([b0, b0], axis=0)
    m0 = jnp.max(s, axis=-1, keepdims=True)
    p = jnp.exp(s - m0)
    m_ref[...] = m0
    l_ref[...] = jnp.sum(p, axis=-1, keepdims=True)
    acc_ref[...] = jnp.dot(p.astype(BF16), vc, preferred_element_type=F32)

    @pl.when(qi > 0)
    def _():
        s, vc = scores(qi - 1)
        b1 = bias_ref[0, 1]
        update(s + jnp.concatenate([b1, b1], axis=0), vc)

    def far(ki, carry):
        update(*scores(ki))
        return carry

    lax.fori_loop(0, jnp.maximum(qi - 1, 0), far, 0)

    lam = _lambda(lq1_ref[...], lk1_ref[...], lq2_ref[...], lk2_ref[...], lam_init)
    on = acc_ref[...] / l_ref[...]
    o = on[:tq] - lam * on[tq:]
    o = o * lax.rsqrt(jnp.mean(o * o, axis=-1, keepdims=True) + RMS_EPS)
    o_ref[0] = (o * g_ref[...] * (1.0 - lam_init)).astype(BF16)


def _attn_prompt_call(qb, kb, vb, rel_bias, subln_g, lams, lam_init, *, tq):
    bsz, t, d = qb.shape
    hw = 2 * HEAD_DIM
    nh = d // hw
    assert hw == LANE and t % tq == 0 and tq >= MAX_DISTANCE
    i = jnp.arange(tq, dtype=jnp.int32)
    d0 = i[:, None] - i[None, :]
    tile0 = jnp.where((d0 >= 0)[None], _bias_by_distance(rel_bias, d0), NEG_BIG)
    tile1 = _bias_by_distance(rel_bias, d0 + tq)
    bias = jnp.stack([tile0, tile1], axis=1)
    row = lambda: pl.BlockSpec((1, HEAD_DIM), lambda b, h, qi: (0, 0))
    body = functools.partial(_attn_prompt_body, tq=tq, lam_init=lam_init)
    return pl.pallas_call(
        body, grid=(bsz, nh, t // tq),
        in_specs=[pl.BlockSpec((1, tq, hw), lambda b, h, qi: (b, qi, h)),
                  pl.BlockSpec((1, t, hw), lambda b, h, qi: (b, 0, h)),
                  pl.BlockSpec((1, t, hw), lambda b, h, qi: (b, 0, h)),
                  pl.BlockSpec((1, 2, tq, tq), lambda b, h, qi: (h, 0, 0, 0)),
                  pl.BlockSpec((1, hw), lambda b, h, qi: (0, 0)),
                  row(), row(), row(), row()],
        out_specs=pl.BlockSpec((1, tq, hw), lambda b, h, qi: (b, qi, h)),
        out_shape=jax.ShapeDtypeStruct((bsz, t, d), BF16),
        scratch_shapes=[pltpu.VMEM((2 * tq, 1), F32), pltpu.VMEM((2 * tq, 1), F32),
                        pltpu.VMEM((2 * tq, hw), F32)],
        compiler_params=_cparams(3), name="attn_prompt")(
            qb, kb, vb, bias, subln_g.reshape(1, hw).astype(F32), *lams)


def _attn_sample_body(pt_ref, q_ref, kn_ref, vn_ref, bias_ref, g_ref,
                      lq1_ref, lk1_ref, lq2_ref, lk2_ref, *rest, pages_per_step, nh, lam_init):
    k_refs = rest[:pages_per_step]
    v_refs = rest[pages_per_step:2 * pages_per_step]
    o_ref, m_ref, l_ref, acc_ref = rest[2 * pages_per_step:]
    del pt_ref
    step = pl.program_id(1)
    last = pl.num_programs(1) - 1
    d = q_ref.shape[-1]
    hw = d // nh
    nr = 2 * nh
    page = k_refs[0].shape[1]

    rid = lax.broadcasted_iota(jnp.int32, (nr, d), 0)
    lid = lax.broadcasted_iota(jnp.int32, (nr, d), 1)
    sel = (lid // HEAD_DIM) == (2 * (rid % nh) + rid // nh)
    qrows = jnp.where(sel, jnp.broadcast_to(q_ref[0], (nr, d)), 0.0)
    qrows_b = qrows.astype(BF16)

    @pl.when(step == 0)
    def _():
        m_ref[...] = jnp.full(m_ref.shape, NEG_BIG, F32)
        l_ref[...] = jnp.zeros(l_ref.shape, F32)
        acc_ref[...] = jnp.zeros(acc_ref.shape, F32)

    def update(s, pv_fn):
        m_prev = m_ref[...]
        m_new = jnp.maximum(m_prev, jnp.max(s, axis=-1, keepdims=True))
        alpha = jnp.exp(m_prev - m_new)
        p = jnp.exp(s - m_new)
        l_ref[...] = alpha * l_ref[...] + jnp.sum(p, axis=-1, keepdims=True)
        acc_ref[...] = alpha * acc_ref[...] + pv_fn(p)
        m_ref[...] = m_new

    for j in range(pages_per_step):
        kp = k_refs[j][0].astype(BF16)
        vp = v_refs[j][0].astype(BF16)
        s = lax.dot_general(qrows_b, kp, (((1,), (1,)), ((), ())), preferred_element_type=F32)
        if j == pages_per_step - 1:
            s = s + jnp.where(step == last, bias_ref[:, 0:page], 0.0)
        update(s, lambda p, vp=vp: jnp.dot(p.astype(BF16), vp, preferred_element_type=F32))

    @pl.when(step == last)
    def _():
        kn = kn_ref[0].astype(BF16).astype(F32)
        vn = vn_ref[0].astype(BF16).astype(F32)
        s_new = jnp.sum(qrows_b.astype(F32) * kn, axis=-1, keepdims=True)
        s_new = s_new + bias_ref[:, page:page + 1]
        update(s_new, lambda p: p * vn)
        lam = _lambda(lq1_ref[...], lk1_ref[...], lq2_ref[...], lk2_ref[...], lam_init)
        on = acc_ref[...] / l_ref[...]
        diag = lambda c: jnp.concatenate(
            [on[c * nh + h:c * nh + h + 1, h * hw:(h + 1) * hw] for h in range(nh)], axis=0)
        o = diag(0) - lam * diag(1)
        o = o * lax.rsqrt(jnp.mean(o * o, axis=-1, keepdims=True) + RMS_EPS)
        o_ref[0] = (o * g_ref[...] * (1.0 - lam_init)).astype(BF16)


def _attn_sample_call(q, k_new, v_new, cache_k_l, cache_v_l, page_table, rel_bias, subln_g,
                      lams, lam_init, *, pages_per_step):
    s_cnt, _, d = q.shape
    n_pages = page_table.shape[1]
    page = cache_k_l.shape[1]
    hw = 2 * HEAD_DIM
    nh = d // hw
    assert n_pages % pages_per_step == 0 and page >= MAX_DISTANCE and hw == LANE
    dist = jnp.concatenate([page - jnp.arange(page, dtype=jnp.int32),
                            jnp.zeros((LANE,), jnp.int32)])
    bias = jnp.tile(_bias_by_distance(rel_bias, dist), (2, 1))
    pt_flat = page_table.reshape(-1).astype(jnp.int32)

    def page_spec(j):
        return pl.BlockSpec(
            (1, page, d),
            lambda s, p, pt: (pt[s * n_pages + p * pages_per_step + j], 0, 0))

    row = lambda: pl.BlockSpec((1, HEAD_DIM), lambda s, p, pt: (0, 0))
    tok = lambda: pl.BlockSpec((1, 1, d), lambda s, p, pt: (s, 0, 0))
    grid_spec = pltpu.PrefetchScalarGridSpec(
        num_scalar_prefetch=1, grid=(s_cnt, n_pages // pages_per_step),
        in_specs=[tok(), tok(), tok(),
                  pl.BlockSpec(bias.shape, lambda s, p, pt: (0, 0)),
                  pl.BlockSpec((1, hw), lambda s, p, pt: (0, 0)),
                  row(), row(), row(), row()]
                 + [page_spec(j) for j in range(pages_per_step)] * 2,
        out_specs=pl.BlockSpec((1, nh, hw), lambda s, p, pt: (s, 0, 0)),
        scratch_shapes=[pltpu.VMEM((2 * nh, 1), F32), pltpu.VMEM((2 * nh, 1), F32),
                        pltpu.VMEM((2 * nh, d), F32)])
    body = functools.partial(_attn_sample_body, pages_per_step=pages_per_step, nh=nh,
                             lam_init=lam_init)
    o = pl.pallas_call(
        body, grid_spec=grid_spec, out_shape=jax.ShapeDtypeStruct((s_cnt, nh, hw), BF16),
        compiler_params=_cparams(2), name="attn_sample")(
            pt_flat, q, k_new, v_new, bias, subln_g.reshape(1, hw).astype(F32), *lams,
            *([cache_k_l] * pages_per_step), *([cache_v_l] * pages_per_step))
    return o.reshape(s_cnt, d)


def _merge_body(o_ref, yc_ref, ga_ref, x_ref, wao_ref, wo_ref, g_ref, b_ref, x1_ref, *, alpha):
    y_attn = jnp.dot(o_ref[...], wao_ref[...], preferred_element_type=F32)
    merged = yc_ref[...] + _sigmoid(ga_ref[...]) * y_attn
    mix = jnp.dot(merged.astype(BF16), wo_ref[...], preferred_element_type=F32)
    x1_ref[...] = _layer_norm(alpha * x_ref[...] + mix, g_ref[...], b_ref[...])


def _merge_call(o_b, yc, ga, x, w_attn_out_b, w_o_b, ln_g, ln_b, alpha, *, tb):
    n, d = x.shape
    tb = min(tb, n)
    assert n % tb == 0
    blk = lambda: pl.BlockSpec((tb, d), lambda i: (i, 0))
    return pl.pallas_call(
        functools.partial(_merge_body, alpha=alpha), grid=(n // tb,),
        in_specs=[blk(), blk(), blk(), blk(), _resident((d, d)), _resident((d, d)),
                  _resident((1, d)), _resident((1, d))],
        out_specs=blk(), out_shape=jax.ShapeDtypeStruct((n, d), F32),
        compiler_params=_cparams(1), name="merge_ln")(
            o_b, yc, ga, x, w_attn_out_b, w_o_b, ln_g.reshape(1, d), ln_b.reshape(1, d))


def _gelu_tanh(x):
    c = math.sqrt(2.0 / math.pi)
    return 0.5 * x * (1.0 + jnp.tanh(c * (x + 0.044715 * (x * x * x))))


def _top_values(s, n):
    vals = []
    cur = s
    for r in range(n):
        m = jnp.max(cur, axis=0, keepdims=True)
        vals.append(m)
        if r + 1 < n:
            cur = jnp.where(cur >= m, -jnp.inf, cur)
    return vals


def _peer_body(x1_ref, p_ref, wqt_ref, keys_ref, u_ref, vt_ref, g_ref, b_ref, pg_ref, pp_ref,
               out_ref, xb_ref, th_ref, al_ref, s2_ref, be_ref, ta_ref, tb_ref, yt_ref, wt_ref,
               *, nh, nk, dkh, alpha, rows_per_chunk):
    e = pl.program_id(1)
    tb = x1_ref.shape[0]
    n_top = PEER_TOPK + 1

    @pl.when(e == 0)
    def _():
        xb = x1_ref[...].astype(BF16)
        xb_ref[...] = xb
        qt = lax.dot_general(wqt_ref[...], xb, (((1,), (1,)), ((), ())),
                             preferred_element_type=F32)
        for h in range(nh):
            base = h * 2 * dkh
            s1 = jnp.dot(keys_ref[h, 0], qt[base:base + dkh], precision=lax.Precision.HIGHEST,
                         preferred_element_type=F32)
            s2 = jnp.dot(keys_ref[h, 1], qt[base + dkh:base + 2 * dkh],
                         precision=lax.Precision.HIGHEST, preferred_element_type=F32)
            th_ref[h] = s1
            s2_ref[h] = s2
            for r, (a, b) in enumerate(zip(_top_values(s1, n_top), _top_values(s2, n_top))):
                ta_ref[r, h:h + 1, :] = a
                tb_ref[r, h:h + 1, :] = b
        cands = [ta_ref[k] + tb_ref[l] for k in range(n_top) for l in range(n_top)
                 if (k + 1) * (l + 1) <= n_top]
        best = []
        for r in range(n_top):
            m = functools.reduce(jnp.maximum, cands)
            best.append(m)
            if r + 1 < n_top:
                cands = [jnp.where(c >= m, -jnp.inf, c) for c in cands]
        z = functools.reduce(lambda a, b: a + b,
                             [jnp.exp(c - best[0]) for c in best[:PEER_TOPK]])
        tau = 0.5 * (best[PEER_TOPK - 1] + best[PEER_TOPK])
        inv_z = 1.0 / z
        a1 = ta_ref[0]
        b1 = tb_ref[0]
        for h in range(nh):
            s1 = th_ref[h]
            al_ref[h] = jnp.exp(s1 - a1[h:h + 1]) * inv_z[h:h + 1]
            th_ref[h] = tau[h:h + 1] - s1
            be_ref[h] = jnp.exp(s2_ref[h] - b1[h:h + 1])
        yt_ref[...] = jnp.zeros(yt_ref.shape, F32)

    at = lax.dot_general(u_ref[...], xb_ref[...], (((1,), (1,)), ((), ())),
                         preferred_element_type=F32)
    i0 = pl.multiple_of(e * rows_per_chunk, rows_per_chunk)
    for il in range(rows_per_chunk):
        gsum = jnp.zeros((nk, tb), F32)
        for h in range(nh):
            th = th_ref[h, pl.ds(i0, rows_per_chunk), :][il:il + 1]
            al = al_ref[h, pl.ds(i0, rows_per_chunk), :][il:il + 1]
            gsum = gsum + jnp.where(s2_ref[h] >= th, be_ref[h], 0.0) * al
        act = _gelu_tanh(at[il * nk:(il + 1) * nk])
        wt_ref[il * nk:(il + 1) * nk, :] = (gsum * act).astype(BF16)
    yt_ref[...] += jnp.dot(vt_ref[...], wt_ref[...], preferred_element_type=F32)

    @pl.when(e == pl.num_programs(1) - 1)
    def _():
        x1 = x1_ref[...]
        x2 = _layer_norm(alpha * x1 + yt_ref[...].T, g_ref[...], b_ref[...])
        gate = _sigmoid(jnp.dot(x2.astype(BF16), pg_ref[...], preferred_element_type=F32))
        emb = jnp.dot(p_ref[...].astype(BF16), pp_ref[...], preferred_element_type=F32)
        out_ref[...] = x2 + gate * emb


def _peer_call(x1, p, wq_t_b, keys, u_b, v_t_b, ln_g, ln_b, ple_gate_b, ple_proj_b, alpha,
               *, tb, rows_per_chunk):
    n, d = x1.shape
    nh, _, nk, dkh = keys.shape
    n_exp = u_b.shape[0]
    assert n_exp == nk * nk and nk % rows_per_chunk == 0 and rows_per_chunk == SUBLANE
    tb = min(tb, n)
    assert n % tb == 0 and tb % LANE == 0
    ec = rows_per_chunk * nk
    dp = p.shape[1]
    n_top = PEER_TOPK + 1
    body = functools.partial(_peer_body, nh=nh, nk=nk, dkh=dkh, alpha=alpha,
                             rows_per_chunk=rows_per_chunk)
    per_head = lambda: pltpu.VMEM((nh, nk, tb), F32)
    return pl.pallas_call(
        body, grid=(n // tb, n_exp // ec),
        in_specs=[pl.BlockSpec((tb, d), lambda i, e: (i, 0)),
                  pl.BlockSpec((tb, dp), lambda i, e: (i, 0)),
                  _resident(wq_t_b.shape), _resident(keys.shape),
                  pl.BlockSpec((ec, d), lambda i, e: (e, 0)),
                  pl.BlockSpec((d, ec), lambda i, e: (0, e)),
                  _resident((1, d)), _resident((1, d)),
                  _resident(ple_gate_b.shape), _resident(ple_proj_b.shape)],
        out_specs=pl.BlockSpec((tb, d), lambda i, e: (i, 0)),
        out_shape=jax.ShapeDtypeStruct((n, d), F32),
        scratch_shapes=[pltpu.VMEM((tb, d), BF16), per_head(), per_head(), per_head(),
                        per_head(), pltpu.VMEM((n_top, nh, tb), F32),
                        pltpu.VMEM((n_top, nh, tb), F32), pltpu.VMEM((d, tb), F32),
                        pltpu.VMEM((ec, tb), BF16)],
        compiler_params=_cparams(2), name="peer_ln_ple")(
            x1, p, wq_t_b, keys.astype(F32), u_b, v_t_b, ln_g.reshape(1, d), ln_b.reshape(1, d),
            ple_gate_b, ple_proj_b)


def _tile_sizes(seq):
    pick = lambda pref: max(t for t in (pref, 256, 128) if t <= pref and seq % t == 0)
    return dict(inproj_tb=pick(256), attn_tq=pick(256), merge_tb=pick(512), peer_tb=pick(512))


def kernel(x_prompt, x_sample, p_prompt, p_sample, cache_k, cache_v, state_conv, page_table,
           rel_bias, w_in, conv_w, w_conv_out, w_attn_out, w_o, lambda_q1, lambda_k1,
           lambda_q2, lambda_k2, subln_g, ln1_g, ln1_b, ln2_g, ln2_b, peer_wq, peer_keys,
           peer_u, peer_v, ple_proj, ple_gate):
    depth = w_in.shape[0]
    bsz, seq, d = x_prompt.shape
    s_cnt, dec_seq, _ = x_sample.shape
    assert dec_seq == 1
    nh = d // (2 * HEAD_DIM)
    alpha = (2.0 * depth) ** 0.25
    tiles = _tile_sizes(seq)
    n_phys, page = cache_k.shape[1], cache_k.shape[2]

    y_p, y_s = x_prompt, x_sample
    outs = [[] for _ in range(6)]
    for i in range(depth):
        lam_init = 0.8 - 0.6 * math.exp(-0.3 * i)
        lams = [a[i].reshape(1, HEAD_DIM).astype(F32)
                for a in (lambda_q1, lambda_k1, lambda_q2, lambda_k2)]
        w_in_b = w_in[i].astype(BF16)
        wco_b = w_conv_out[i].astype(BF16)
        wao_b = w_attn_out[i].astype(BF16)
        wo_b = w_o[i].astype(BF16)
        wq_t_b = peer_wq[i].T.astype(BF16)
        u_b = peer_u[i].astype(BF16)
        v_t_b = peer_v[i].T.astype(BF16)
        pg_b = ple_gate[i].astype(BF16)
        pp_b = ple_proj[i].astype(BF16)

        def tail(o_b, yc, ga, x, p):
            n = x.shape[0]
            x1 = _merge_call(o_b, yc.reshape(n, d), ga.reshape(n, d), x, wao_b, wo_b,
                             ln1_g[i], ln1_b[i], alpha, tb=tiles["merge_tb"])
            return _peer_call(x1, p, wq_t_b, peer_keys[i], u_b, v_t_b, ln2_g[i], ln2_b[i],
                              pg_b, pp_b, alpha, tb=tiles["peer_tb"], rows_per_chunk=SUBLANE)

        k_p, v_p, qb, kb, vb, yc, ga, c_p = _inproj_call(
            y_p, w_in_b, conv_w[i], wco_b, tb=tiles["inproj_tb"])
        o_b = _attn_prompt_call(qb, kb, vb, rel_bias, subln_g[i], lams, lam_init,
                                tq=tiles["attn_tq"])
        y_p = tail(o_b.reshape(bsz * seq, d), yc, ga, y_p.reshape(bsz * seq, d),
                   p_prompt[i].reshape(bsz * seq, -1)).reshape(bsz, seq, d)

        k_s, v_s, qb, _, _, yc, ga, u_s = _inproj_call(
            y_s, w_in_b, conv_w[i], wco_b, state=state_conv[i], tb=s_cnt)
        k_s = k_s.reshape(s_cnt, 1, d)
        v_s = v_s.reshape(s_cnt, 1, d)
        o_b = _attn_sample_call(
            qb.reshape(s_cnt, 1, d).astype(F32), k_s, v_s, cache_k[i].reshape(n_phys, page, d),
            cache_v[i].reshape(n_phys, page, d), page_table, rel_bias, subln_g[i], lams,
            lam_init, pages_per_step=4)
        c_s = jnp.stack([state_conv[i][:, 1, :], u_s.reshape(s_cnt, d)], axis=1)
        y_s = tail(o_b, yc, ga, y_s.reshape(s_cnt, d),
                   p_sample[i].reshape(s_cnt, -1)).reshape(s_cnt, 1, d)

        outs[0].append(k_p.reshape(bsz, seq, nh, 2, HEAD_DIM))
        outs[1].append(v_p.reshape(bsz, seq, nh, 2 * HEAD_DIM))
        outs[2].append(c_p)
        outs[3].append(k_s.reshape(s_cnt, 1, nh, 2, HEAD_DIM))
        outs[4].append(v_s.reshape(s_cnt, 1, nh, 2 * HEAD_DIM))
        outs[5].append(c_s)
    return (y_p, y_s) + tuple(jnp.stack(o) for o in outs)
```

```python
import functools
import math

import jax
import jax.numpy as jnp
from jax import lax
from jax.experimental import pallas as pl
from jax.experimental.pallas import tpu as pltpu

F32 = jnp.float32
BF16 = jnp.bfloat16

HEAD_DIM = 64
MAX_DISTANCE = 128
PEER_TOPK = 16
LN_EPS = 1e-5
RMS_EPS = 1e-5
NEG_BIG = -1e30
LOG2E = 1.4426950408889634
LANE = 128
SUBLANE = 8
VMEM_LIMIT_BYTES = 56 * 1024 * 1024


def _cparams(n_axes):
    return pltpu.CompilerParams(dimension_semantics=("arbitrary",) * n_axes,
                                vmem_limit_bytes=VMEM_LIMIT_BYTES)


def _resident(shape):
    zeros = (0,) * len(shape)
    return pl.BlockSpec(shape, lambda *_: zeros, pipeline_mode=pl.Buffered(1))


def _sigmoid(x):
    return 1.0 / (1.0 + jnp.exp(-x))


def _layer_norm(x, g, b):
    mu = jnp.mean(x, axis=-1, keepdims=True)
    xc = x - mu
    var = jnp.mean(xc * xc, axis=-1, keepdims=True)
    return xc * lax.rsqrt(var + LN_EPS) * g + b


def _lambda(lq1, lk1, lq2, lk2, lam_init):
    s1 = jnp.sum(lq1 * lk1, axis=-1, keepdims=True)
    s2 = jnp.sum(lq2 * lk2, axis=-1, keepdims=True)
    return jnp.exp(s1) - jnp.exp(s2) + lam_init


def _inproj_body(*refs, d, tb, conv_w_len, shifted_inputs, q_scale):
    if shifted_inputs:
        (x_ref, w_ref, cw_ref, wco_ref, um2_ref, um1_ref,
         k_ref, v_ref, qb_ref, kb_ref, vb_ref, yc_ref, ga_ref, u_ref) = refs
    else:
        (x_ref, w_ref, cw_ref, wco_ref,
         k_ref, v_ref, qb_ref, kb_ref, vb_ref, yc_ref, ga_ref, nc_ref, ubuf_ref) = refs
    assert conv_w_len == 3
    xb = x_ref[0].astype(BF16)

    def proj(g):
        return jnp.dot(xb, w_ref[:, g * d:(g + 1) * d], preferred_element_type=F32)

    u = proj(2) * proj(0)
    if shifted_inputs:
        um2 = um2_ref[0]
        um1 = um1_ref[0]
        u_ref[0] = u
    else:
        @pl.when(pl.program_id(1) == 0)
        def _():
            ubuf_ref[0:SUBLANE, :] = jnp.zeros((SUBLANE, d), F32)
        ubuf_ref[SUBLANE:SUBLANE + tb, :] = u
        um1 = ubuf_ref[SUBLANE - 1:SUBLANE - 1 + tb, :]
        um2 = ubuf_ref[SUBLANE - 2:SUBLANE - 2 + tb, :]
        ubuf_ref[0:SUBLANE, :] = u[tb - SUBLANE:tb, :]
        nc_ref[0] = u[tb - 2:tb, :]
    conv = um2 * cw_ref[0:1, :] + um1 * cw_ref[1:2, :] + u * cw_ref[2:3, :]
    y_conv = jnp.dot((proj(1) * conv).astype(BF16), wco_ref[...], preferred_element_type=F32)
    yc_ref[0] = _sigmoid(proj(6)) * y_conv
    ga_ref[0] = proj(7)
    qb_ref[0] = (proj(3) * q_scale).astype(BF16)
    k = proj(4)
    k_ref[0] = k
    kb_ref[0] = k.astype(BF16)
    v = proj(5)
    v_ref[0] = v
    vb_ref[0] = v.astype(BF16)


def _inproj_call(x, w_in_b, conv_w, w_conv_out_b, state=None, *, tb):
    bsz, t, d = x.shape
    assert w_in_b.shape == (d, 8 * d)
    q_scale = HEAD_DIM ** -0.5 * LOG2E
    shifted = state is not None
    if shifted:
        assert t == 1
        xr = x.reshape(1, bsz, d)
        rows, nb, nt = bsz, 1, 1
        tb = bsz
    else:
        xr = x
        rows, nb, nt = t, bsz, t // tb
        assert t % tb == 0 and tb % SUBLANE == 0
    blk = lambda: pl.BlockSpec((1, tb, d), lambda b, i: (b, i, 0))
    in_specs = [blk(), _resident((d, 8 * d)), _resident(conv_w.shape), _resident((d, d))]
    args = [xr, w_in_b, conv_w, w_conv_out_b]
    f32_out = jax.ShapeDtypeStruct((nb, rows, d), F32)
    bf_out = jax.ShapeDtypeStruct((nb, rows, d), BF16)
    out_shape = [f32_out, f32_out, bf_out, bf_out, bf_out, f32_out, f32_out]
    out_specs = [blk() for _ in range(7)]
    scratch = []
    if shifted:
        in_specs += [blk(), blk()]
        args += [state[:, 0, :].reshape(1, bsz, d), state[:, 1, :].reshape(1, bsz, d)]
        out_shape.append(f32_out)
        out_specs.append(blk())
    else:
        out_shape.append(jax.ShapeDtypeStruct((nb, 2, d), F32))
        out_specs.append(pl.BlockSpec((1, 2, d), lambda b, i: (b, 0, 0)))
        scratch.append(pltpu.VMEM((tb + SUBLANE, d), F32))
    body = functools.partial(_inproj_body, d=d, tb=tb, conv_w_len=conv_w.shape[0],
                             shifted_inputs=shifted, q_scale=q_scale)
    return pl.pallas_call(
        body, grid=(nb, nt), in_specs=in_specs, out_specs=out_specs, out_shape=out_shape,
        scratch_shapes=scratch, compiler_params=_cparams(2), name="inproj_conv")(*args)


def _t5_bucket(n, num_buckets):
    n = jnp.maximum(n, 0)
    max_exact = num_buckets // 2
    nf = jnp.maximum(n, 1).astype(F32)
    large = max_exact + (jnp.log(nf / max_exact) / math.log(MAX_DISTANCE / max_exact)
                         * (num_buckets - max_exact)).astype(jnp.int32)
    large = jnp.minimum(large, num_buckets - 1)
    return jnp.where(n < max_exact, n, large)


def _bias_by_distance(rel_bias, dist):
    nbk = rel_bias.shape[0]
    rb = (rel_bias.astype(F32) - rel_bias.astype(F32)[nbk - 1][None, :]) * LOG2E
    onehot = (_t5_bucket(dist, nbk)[..., None] == jnp.arange(nbk, dtype=jnp.int32)).astype(F32)
    return jnp.moveaxis(jnp.dot(onehot, rb, precision=lax.Precision.HIGHEST), -1, 0)


def _toeplitz(w, n):
    h = w.shape[0]
    r = jnp.concatenate([w[:, ::-1], jnp.zeros((h, 1), w.dtype)], axis=1)
    flat = jnp.tile(r, (1, n))
    return flat[:, n - 1:n - 1 + n * (2 * n - 1)].reshape(h, n, 2 * n - 1)[:, :, :n]


def _attn_prompt_body(q_ref, k_ref, v_ref, bias_ref, g_ref, lq1_ref, lk1_ref, lq2_ref, lk2_ref,
                      o_ref, qq_ref, m_ref, l_ref, acc_ref, *, tq, rsub, lam_init):
    qi = pl.program_id(2)
    rows = 2 * tq
    nlt = tq // LANE
    q = q_ref[0]
    lane = lax.broadcasted_iota(jnp.int32, q.shape, 1)
    zero = jnp.zeros_like(q)
    qq_ref[0:tq, :] = jnp.where(lane < HEAD_DIM, q, zero)
    qq_ref[tq:rows, :] = jnp.where(lane >= HEAD_DIM, q, zero)

    def chunk(ki, bias_idx, first):
        start = pl.multiple_of(ki * tq, tq)
        kc = k_ref[0, pl.ds(start, tq), :]
        vc = v_ref[0, pl.ds(start, tq), :]
        for r0 in range(0, rows, rsub):
            rs = slice(r0, r0 + rsub)
            s = lax.dot_general(qq_ref[rs, :], kc, (((1,), (1,)), ((), ())),
                                preferred_element_type=F32)
            if bias_idx is not None:
                b0 = r0 % tq
                s = s + bias_ref[0, bias_idx, b0:b0 + rsub, :]
            tiles = [s[:, j * LANE:(j + 1) * LANE] for j in range(nlt)]
            mx = jnp.max(functools.reduce(jnp.maximum, tiles), axis=-1, keepdims=True)
            mx = jnp.broadcast_to(mx, (rsub, LANE))
            if first:
                m_new = mx
            else:
                m_prev = m_ref[rs, :]
                m_new = jnp.maximum(m_prev, mx)
                alpha = jnp.exp2(m_prev - m_new)
            ps = [jnp.exp2(t - m_new) for t in tiles]
            psum = functools.reduce(lambda a, b: a + b, ps)
            pv = jnp.dot(jnp.concatenate(ps, axis=1).astype(BF16), vc,
                         preferred_element_type=F32)
            if first:
                l_ref[rs, :] = psum
                acc_ref[rs, :] = pv
            else:
                l_ref[rs, :] = alpha * l_ref[rs, :] + psum
                acc_ref[rs, :] = alpha * acc_ref[rs, :] + pv
            m_ref[rs, :] = m_new

    chunk(qi, 0, True)

    @pl.when(qi > 0)
    def _():
        chunk(qi - 1, 1, False)

    def far(ki, carry):
        chunk(ki, None, False)
        return carry

    lax.fori_loop(0, jnp.maximum(qi - 1, 0), far, 0)

    lam = _lambda(lq1_ref[...], lk1_ref[...], lq2_ref[...], lk2_ref[...], lam_init)
    on = acc_ref[...] / jnp.sum(l_ref[...], axis=-1, keepdims=True)
    o = on[:tq] - lam * on[tq:]
    o = o * lax.rsqrt(jnp.mean(o * o, axis=-1, keepdims=True) + RMS_EPS)
    o_ref[0] = (o * g_ref[...] * (1.0 - lam_init)).astype(BF16)


def _attn_prompt_call(qb, kb, vb, rel_bias, subln_g, lams, lam_init, *, tq, rsub):
    bsz, t, d = qb.shape
    hw = 2 * HEAD_DIM
    nh = d // hw
    assert hw == LANE and t % tq == 0 and tq >= MAX_DISTANCE and tq % rsub == 0
    dd = jnp.arange(1 - tq, tq, dtype=jnp.int32)
    w0 = jnp.where((dd >= 0)[None], _bias_by_distance(rel_bias, dd), NEG_BIG)
    w1 = _bias_by_distance(rel_bias, dd + tq)
    bias = jnp.stack([_toeplitz(w0, tq), _toeplitz(w1, tq)], axis=1)
    row = lambda: pl.BlockSpec((1, HEAD_DIM), lambda b, h, qi: (0, 0))
    body = functools.partial(_attn_prompt_body, tq=tq, rsub=rsub, lam_init=lam_init)
    stat = lambda: pltpu.VMEM((2 * tq, hw), F32)
    return pl.pallas_call(
        body, grid=(bsz, nh, t // tq),
        in_specs=[pl.BlockSpec((1, tq, hw), lambda b, h, qi: (b, qi, h)),
                  pl.BlockSpec((1, t, hw), lambda b, h, qi: (b, 0, h)),
                  pl.BlockSpec((1, t, hw), lambda b, h, qi: (b, 0, h)),
                  pl.BlockSpec((1, 2, tq, tq), lambda b, h, qi: (h, 0, 0, 0)),
                  pl.BlockSpec((1, hw), lambda b, h, qi: (0, 0)),
                  row(), row(), row(), row()],
        out_specs=pl.BlockSpec((1, tq, hw), lambda b, h, qi: (b, qi, h)),
        out_shape=jax.ShapeDtypeStruct((bsz, t, d), BF16),
        scratch_shapes=[pltpu.VMEM((2 * tq, hw), BF16), stat(), stat(), stat()],
        compiler_params=_cparams(3), name="attn_prompt")(
            qb, kb, vb, bias, subln_g.reshape(1, hw).astype(F32), *lams)


def _attn_sample_body(pt_ref, q_ref, kn_ref, vn_ref, bias_ref, g_ref, expand_ref,
                      lq1_ref, lk1_ref, lq2_ref, lk2_ref, *rest, pages_per_step, nh, lam_init):
    k_refs = rest[:pages_per_step]
    v_refs = rest[pages_per_step:2 * pages_per_step]
    o_ref, m_ref, l_ref, acc_ref = rest[2 * pages_per_step:]
    del pt_ref
    step = pl.program_id(1)
    last = pl.num_programs(1) - 1
    d = q_ref.shape[-1]
    nr = 2 * nh
    page = k_refs[0].shape[2]

    rid = lax.broadcasted_iota(jnp.int32, (nr, d), 0)
    lid = lax.broadcasted_iota(jnp.int32, (nr, d), 1)
    sel = (lid // HEAD_DIM) == (2 * (rid % nh) + rid // nh)
    qrows_b = jnp.where(sel, jnp.broadcast_to(q_ref[0], (nr, d)), 0.0).astype(BF16)

    @pl.when(step == 0)
    def _():
        m_ref[...] = jnp.full(m_ref.shape, NEG_BIG, F32)
        l_ref[...] = jnp.zeros(l_ref.shape, F32)
        acc_ref[...] = jnp.zeros(acc_ref.shape, F32)

    tiles = [jnp.dot(qrows_b, k_refs[j][0].astype(BF16), preferred_element_type=F32)
             for j in range(pages_per_step)]
    tiles[-1] = tiles[-1] + jnp.where(step == last, bias_ref[:, 0:page], 0.0)
    mx = jnp.max(functools.reduce(jnp.maximum, tiles), axis=-1, keepdims=True)
    m_prev = m_ref[...]
    m_new = jnp.maximum(m_prev, mx)
    alpha = jnp.exp2(m_prev - m_new)
    ps = [jnp.exp2(t - m_new) for t in tiles]
    psum = jnp.sum(functools.reduce(lambda a, b: a + b, ps), axis=-1, keepdims=True)
    pe = jnp.dot(jnp.concatenate(ps, axis=0).astype(BF16), expand_ref[...],
                 preferred_element_type=F32)
    rid2 = lax.broadcasted_iota(jnp.int32, pe.shape, 0)
    lid2 = lax.broadcasted_iota(jnp.int32, pe.shape, 1)
    pe = jnp.where((lid2 % nh) == (rid2 % nh), pe, 0.0).astype(BF16)
    pv = alpha * acc_ref[...]
    for j in range(pages_per_step):
        pv = jnp.dot(pe[j * nr:(j + 1) * nr], v_refs[j][0].astype(BF16),
                     preferred_element_type=F32) + pv
    l_new = alpha * l_ref[...] + psum
    m_ref[...] = m_new
    l_ref[...] = l_new
    acc_ref[...] = pv

    @pl.when(step == last)
    def _():
        s_new = jnp.sum(qrows_b.astype(F32) * kn_ref[0], axis=-1, keepdims=True)
        s_new = s_new + bias_ref[:, page:page + 1]
        m_fin = jnp.maximum(m_new, s_new)
        a_fin = jnp.exp2(m_new - m_fin)
        p_new = jnp.exp2(s_new - m_fin)
        vn = vn_ref[0]
        acc = a_fin * pv + p_new * jnp.concatenate([vn, vn], axis=0)
        on = acc / (a_fin * l_new + p_new)
        lam = _lambda(lq1_ref[...], lk1_ref[...], lq2_ref[...], lk2_ref[...], lam_init)
        o = on[:nh] - lam * on[nh:]
        o = o * lax.rsqrt(jnp.mean(o * o, axis=-1, keepdims=True) + RMS_EPS)
        o_ref[0] = (o * g_ref[...] * (1.0 - lam_init)).astype(BF16)


def _attn_sample_call(q, k_new, v_new, cache_kt, cache_v2, page_table, rel_bias, subln_g,
                      lams, lam_init, *, pages_per_step):
    s_cnt, _, d = q.shape
    n_pages = page_table.shape[1]
    page = cache_kt.shape[2]
    hw = 2 * HEAD_DIM
    nh = d // hw
    assert n_pages % pages_per_step == 0 and page >= MAX_DISTANCE and hw == LANE
    dist = jnp.concatenate([page - jnp.arange(page, dtype=jnp.int32),
                            jnp.zeros((LANE,), jnp.int32)])
    bias = jnp.tile(_bias_by_distance(rel_bias, dist), (2, 1))
    expand = jnp.repeat(jnp.eye(page, dtype=BF16), nh, axis=1)
    pt_flat = page_table.reshape(-1).astype(jnp.int32)

    def page_spec(j, shape):
        return pl.BlockSpec(
            (1,) + shape,
            lambda s, p, pt: (pt[s * n_pages + p * pages_per_step + j], 0, 0))

    row = lambda: pl.BlockSpec((1, HEAD_DIM), lambda s, p, pt: (0, 0))
    tok = lambda: pl.BlockSpec((1, 1, d), lambda s, p, pt: (s, 0, 0))
    grid_spec = pltpu.PrefetchScalarGridSpec(
        num_scalar_prefetch=1, grid=(s_cnt, n_pages // pages_per_step),
        in_specs=[tok(), tok(), pl.BlockSpec((1, nh, hw), lambda s, p, pt: (s, 0, 0)),
                  pl.BlockSpec(bias.shape, lambda s, p, pt: (0, 0)),
                  pl.BlockSpec((1, hw), lambda s, p, pt: (0, 0)),
                  pl.BlockSpec(expand.shape, lambda s, p, pt: (0, 0)),
                  row(), row(), row(), row()]
                 + [page_spec(j, (d, page)) for j in range(pages_per_step)]
                 + [page_spec(j, (page * nh, hw)) for j in range(pages_per_step)],
        out_specs=pl.BlockSpec((1, nh, hw), lambda s, p, pt: (s, 0, 0)),
        scratch_shapes=[pltpu.VMEM((2 * nh, 1), F32), pltpu.VMEM((2 * nh, 1), F32),
                        pltpu.VMEM((2 * nh, hw), F32)])
    body = functools.partial(_attn_sample_body, pages_per_step=pages_per_step, nh=nh,
                             lam_init=lam_init)
    o = pl.pallas_call(
        body, grid_spec=grid_spec, out_shape=jax.ShapeDtypeStruct((s_cnt, nh, hw), BF16),
        compiler_params=_cparams(2), name="attn_sample")(
            pt_flat, q, k_new, v_new, bias, subln_g.reshape(1, hw).astype(F32), expand, *lams,
            *([cache_kt] * pages_per_step), *([cache_v2] * pages_per_step))
    return o.reshape(s_cnt, d)


def _merge_body(o_ref, yc_ref, ga_ref, x_ref, wao_ref, wo_ref, g_ref, b_ref, x1_ref, *, alpha):
    y_attn = jnp.dot(o_ref[...], wao_ref[...], preferred_element_type=F32)
    merged = yc_ref[...] + _sigmoid(ga_ref[...]) * y_attn
    mix = jnp.dot(merged.astype(BF16), wo_ref[...], preferred_element_type=F32)
    x1_ref[...] = _layer_norm(alpha * x_ref[...] + mix, g_ref[...], b_ref[...])


def _merge_call(o_b, yc, ga, x, w_attn_out_b, w_o_b, ln_g, ln_b, alpha, *, tb):
    n, d = x.shape
    tb = min(tb, n)
    assert n % tb == 0
    blk = lambda: pl.BlockSpec((tb, d), lambda i: (i, 0))
    return pl.pallas_call(
        functools.partial(_merge_body, alpha=alpha), grid=(n // tb,),
        in_specs=[blk(), blk(), blk(), blk(), _resident((d, d)), _resident((d, d)),
                  _resident((1, d)), _resident((1, d))],
        out_specs=blk(), out_shape=jax.ShapeDtypeStruct((n, d), F32),
        compiler_params=_cparams(1), name="merge_ln")(
            o_b, yc, ga, x, w_attn_out_b, w_o_b, ln_g.reshape(1, d), ln_b.reshape(1, d))


def _gelu_tanh(x):
    c = math.sqrt(2.0 / math.pi)
    return 0.5 * x * (1.0 + jnp.tanh(c * (x + 0.044715 * (x * x * x))))


def _top_values(s, n):
    vals = []
    cur = s
    for r in range(n):
        m = jnp.max(cur, axis=0, keepdims=True)
        vals.append(m)
        if r + 1 < n:
            cur = jnp.where(cur >= m, -jnp.inf, cur)
    return vals


def _peer_body(x1_ref, p_ref, wqt_ref, keys_ref, u_ref, vt_ref, g_ref, b_ref, pg_ref, pp_ref,
               out_ref, xb_ref, th_ref, al_ref, s2_ref, be_ref, ta_ref, tb_ref, yt_ref, wt_ref,
               *, nh, nk, dkh, alpha, rows_per_chunk):
    e = pl.program_id(1)
    tb = x1_ref.shape[0]
    n_top = PEER_TOPK + 1

    @pl.when(e == 0)
    def _():
        xb = x1_ref[...].astype(BF16)
        xb_ref[...] = xb
        qt = lax.dot_general(wqt_ref[...], xb, (((1,), (1,)), ((), ())),
                             preferred_element_type=F32)
        for h in range(nh):
            base = h * 2 * dkh
            s1 = jnp.dot(keys_ref[h, 0], qt[base:base + dkh], precision=lax.Precision.HIGHEST,
                         preferred_element_type=F32)
            s2 = jnp.dot(keys_ref[h, 1], qt[base + dkh:base + 2 * dkh],
                         precision=lax.Precision.HIGHEST, preferred_element_type=F32)
            th_ref[h] = s1
            s2_ref[h] = s2
            for r, (a, b) in enumerate(zip(_top_values(s1, n_top), _top_values(s2, n_top))):
                ta_ref[r, h:h + 1, :] = a
                tb_ref[r, h:h + 1, :] = b
        cands = [ta_ref[k] + tb_ref[l] for k in range(n_top) for l in range(n_top)
                 if (k + 1) * (l + 1) <= n_top]
        best = []
        for r in range(n_top):
            m = functools.reduce(jnp.maximum, cands)
            best.append(m)
            if r + 1 < n_top:
                cands = [jnp.where(c >= m, -jnp.inf, c) for c in cands]
        z = functools.reduce(lambda a, b: a + b,
                             [jnp.exp(c - best[0]) for c in best[:PEER_TOPK]])
        tau = 0.5 * (best[PEER_TOPK - 1] + best[PEER_TOPK])
        inv_z = 1.0 / z
        a1 = ta_ref[0]
        b1 = tb_ref[0]
        for h in range(nh):
            s1 = th_ref[h]
            al_ref[h] = jnp.exp(s1 - a1[h:h + 1]) * inv_z[h:h + 1]
            th_ref[h] = tau[h:h + 1] - s1
            be_ref[h] = jnp.exp(s2_ref[h] - b1[h:h + 1])
        yt_ref[...] = jnp.zeros(yt_ref.shape, F32)

    at = lax.dot_general(u_ref[...], xb_ref[...], (((1,), (1,)), ((), ())),
                         preferred_element_type=F32)
    i0 = pl.multiple_of(e * rows_per_chunk, rows_per_chunk)
    for il in range(rows_per_chunk):
        gsum = jnp.zeros((nk, tb), F32)
        for h in range(nh):
            th = th_ref[h, pl.ds(i0, rows_per_chunk), :][il:il + 1]
            al = al_ref[h, pl.ds(i0, rows_per_chunk), :][il:il + 1]
            gsum = gsum + jnp.where(s2_ref[h] >= th, be_ref[h], 0.0) * al
        act = _gelu_tanh(at[il * nk:(il + 1) * nk])
        wt_ref[il * nk:(il + 1) * nk, :] = (gsum * act).astype(BF16)
    yt_ref[...] += jnp.dot(vt_ref[...], wt_ref[...], preferred_element_type=F32)

    @pl.when(e == pl.num_programs(1) - 1)
    def _():
        x1 = x1_ref[...]
        x2 = _layer_norm(alpha * x1 + yt_ref[...].T, g_ref[...], b_ref[...])
        gate = _sigmoid(jnp.dot(x2.astype(BF16), pg_ref[...], preferred_element_type=F32))
        emb = jnp.dot(p_ref[...].astype(BF16), pp_ref[...], preferred_element_type=F32)
        out_ref[...] = x2 + gate * emb


def _peer_call(x1, p, wq_t_b, keys, u_b, v_t_b, ln_g, ln_b, ple_gate_b, ple_proj_b, alpha,
               *, tb, rows_per_chunk):
    n, d = x1.shape
    nh, _, nk, dkh = keys.shape
    n_exp = u_b.shape[0]
    assert n_exp == nk * nk and nk % rows_per_chunk == 0 and rows_per_chunk == SUBLANE
    tb = min(tb, n)
    assert n % tb == 0 and tb % LANE == 0
    ec = rows_per_chunk * nk
    dp = p.shape[1]
    n_top = PEER_TOPK + 1
    body = functools.partial(_peer_body, nh=nh, nk=nk, dkh=dkh, alpha=alpha,
                             rows_per_chunk=rows_per_chunk)
    per_head = lambda: pltpu.VMEM((nh, nk, tb), F32)
    return pl.pallas_call(
        body, grid=(n // tb, n_exp // ec),
        in_specs=[pl.BlockSpec((tb, d), lambda i, e: (i, 0)),
                  pl.BlockSpec((tb, dp), lambda i, e: (i, 0)),
                  _resident(wq_t_b.shape), _resident(keys.shape),
                  pl.BlockSpec((ec, d), lambda i, e: (e, 0)),
                  pl.BlockSpec((d, ec), lambda i, e: (0, e)),
                  _resident((1, d)), _resident((1, d)),
                  _resident(ple_gate_b.shape), _resident(ple_proj_b.shape)],
        out_specs=pl.BlockSpec((tb, d), lambda i, e: (i, 0)),
        out_shape=jax.ShapeDtypeStruct((n, d), F32),
        scratch_shapes=[pltpu.VMEM((tb, d), BF16), per_head(), per_head(), per_head(),
                        per_head(), pltpu.VMEM((n_top, nh, tb), F32),
                        pltpu.VMEM((n_top, nh, tb), F32), pltpu.VMEM((d, tb), F32),
                        pltpu.VMEM((ec, tb), BF16)],
        compiler_params=_cparams(2), name="peer_ln_ple")(
            x1, p, wq_t_b, keys.astype(F32), u_b, v_t_b, ln_g.reshape(1, d), ln_b.reshape(1, d),
            ple_gate_b, ple_proj_b)


def _tile_sizes(seq):
    pick = lambda pref: max(t for t in (pref, 512, 256, 128) if t <= pref and seq % t == 0)
    return dict(inproj_tb=pick(256), attn_tq=pick(1024), attn_rsub=1024, merge_tb=pick(512),
                peer_tb=pick(512), sample_pages_per_step=8)


def kernel(x_prompt, x_sample, p_prompt, p_sample, cache_k, cache_v, state_conv, page_table,
           rel_bias, w_in, conv_w, w_conv_out, w_attn_out, w_o, lambda_q1, lambda_k1,
           lambda_q2, lambda_k2, subln_g, ln1_g, ln1_b, ln2_g, ln2_b, peer_wq, peer_keys,
           peer_u, peer_v, ple_proj, ple_gate):
    depth = w_in.shape[0]
    bsz, seq, d = x_prompt.shape
    s_cnt, dec_seq, _ = x_sample.shape
    assert dec_seq == 1
    nh = d // (2 * HEAD_DIM)
    alpha = (2.0 * depth) ** 0.25
    tiles = _tile_sizes(seq)
    n_phys, page = cache_k.shape[1], cache_k.shape[2]

    y_p, y_s = x_prompt, x_sample
    outs = [[] for _ in range(6)]
    for i in range(depth):
        lam_init = 0.8 - 0.6 * math.exp(-0.3 * i)
        lams = [a[i].reshape(1, HEAD_DIM).astype(F32)
                for a in (lambda_q1, lambda_k1, lambda_q2, lambda_k2)]
        w_in_b = w_in[i].astype(BF16)
        wco_b = w_conv_out[i].astype(BF16)
        wao_b = w_attn_out[i].astype(BF16)
        wo_b = w_o[i].astype(BF16)
        wq_t_b = peer_wq[i].T.astype(BF16)
        u_b = peer_u[i].astype(BF16)
        v_t_b = peer_v[i].T.astype(BF16)
        pg_b = ple_gate[i].astype(BF16)
        pp_b = ple_proj[i].astype(BF16)

        def tail(o_b, yc, ga, x, p):
            n = x.shape[0]
            x1 = _merge_call(o_b, yc.reshape(n, d), ga.reshape(n, d), x, wao_b, wo_b,
                             ln1_g[i], ln1_b[i], alpha, tb=tiles["merge_tb"])
            return _peer_call(x1, p, wq_t_b, peer_keys[i], u_b, v_t_b, ln2_g[i], ln2_b[i],
                              pg_b, pp_b, alpha, tb=tiles["peer_tb"], rows_per_chunk=SUBLANE)

        k_p, v_p, qb, kb, vb, yc, ga, c_p = _inproj_call(
            y_p, w_in_b, conv_w[i], wco_b, tb=tiles["inproj_tb"])
        o_b = _attn_prompt_call(qb, kb, vb, rel_bias, subln_g[i], lams, lam_init,
                                tq=tiles["attn_tq"],
                                rsub=min(tiles["attn_rsub"], tiles["attn_tq"]))
        y_p = tail(o_b.reshape(bsz * seq, d), yc, ga, y_p.reshape(bsz * seq, d),
                   p_prompt[i].reshape(bsz * seq, -1)).reshape(bsz, seq, d)

        k_s, v_s, qb, _, _, yc, ga, u_s = _inproj_call(
            y_s, w_in_b, conv_w[i], wco_b, state=state_conv[i], tb=s_cnt)
        k_s = k_s.reshape(s_cnt, 1, d)
        v_s = v_s.reshape(s_cnt, 1, d)
        cache_kt = jnp.transpose(cache_k[i], (0, 2, 3, 4, 1)).reshape(n_phys, d, page)
        cache_v2 = cache_v[i].reshape(n_phys, page * nh, 2 * HEAD_DIM)
        n_pages = page_table.shape[1]
        pps = max(p for p in range(1, tiles["sample_pages_per_step"] + 1) if n_pages % p == 0)
        o_b = _attn_sample_call(
            qb.reshape(s_cnt, 1, d).astype(F32), k_s, v_s.reshape(s_cnt, nh, 2 * HEAD_DIM),
            cache_kt, cache_v2, page_table, rel_bias, subln_g[i], lams, lam_init,
            pages_per_step=pps)
        c_s = jnp.stack([state_conv[i][:, 1, :], u_s.reshape(s_cnt, d)], axis=1)
        y_s = tail(o_b, yc, ga, y_s.reshape(s_cnt, d),
                   p_sample[i].reshape(s_cnt, -1)).reshape(s_cnt, 1, d)

        outs[0].append(k_p.reshape(bsz, seq, nh, 2, HEAD_DIM))
        outs[1].append(v_p.reshape(bsz, seq, nh, 2 * HEAD_DIM))
        outs[2].append(c_p)
        outs[3].append(k_s.reshape(s_cnt, 1, nh, 2, HEAD_DIM))
        outs[4].append(v_s.reshape(s_cnt, 1, nh, 2 * HEAD_DIM))
        outs[5].append(c_s)
    return (y_p, y_s) + tuple(jnp.stack(o) for o in outs)
```

```python
import functools
import math

import jax
import jax.numpy as jnp
from jax import lax
from jax.experimental import pallas as pl
from jax.experimental.pallas import tpu as pltpu

F32 = jnp.float32
BF16 = jnp.bfloat16

HEAD_DIM = 64
MAX_DISTANCE = 128
PEER_TOPK = 16
LN_EPS = 1e-5
RMS_EPS = 1e-5
NEG_BIG = -1e30
LOG2E = 1.4426950408889634
LANE = 128
SUBLANE = 8
VMEM_LIMIT_BYTES = 56 * 1024 * 1024


def _cparams(n_axes):
    return pltpu.CompilerParams(dimension_semantics=("arbitrary",) * n_axes,
                                vmem_limit_bytes=VMEM_LIMIT_BYTES)


def _resident(shape):
    zeros = (0,) * len(shape)
    return pl.BlockSpec(shape, lambda *_: zeros, pipeline_mode=pl.Buffered(1))


def _sigmoid(x):
    return 1.0 / (1.0 + jnp.exp(-x))


def _layer_norm(x, g, b):
    mu = jnp.mean(x, axis=-1, keepdims=True)
    xc = x - mu
    var = jnp.mean(xc * xc, axis=-1, keepdims=True)
    return xc * lax.rsqrt(var + LN_EPS) * g + b


def _lambda(lq1, lk1, lq2, lk2, lam_init):
    s1 = jnp.sum(lq1 * lk1, axis=-1, keepdims=True)
    s2 = jnp.sum(lq2 * lk2, axis=-1, keepdims=True)
    return jnp.exp(s1) - jnp.exp(s2) + lam_init


def _inproj_body(*refs, d, tb, conv_w_len, shifted_inputs, q_scale):
    if shifted_inputs:
        (x_ref, w_ref, cw_ref, wco_ref, um2_ref, um1_ref,
         k_ref, v_ref, qb_ref, kb_ref, vb_ref, yc_ref, ga_ref, u_ref) = refs
    else:
        (x_ref, w_ref, cw_ref, wco_ref,
         k_ref, v_ref, qb_ref, kb_ref, vb_ref, yc_ref, ga_ref, nc_ref, ubuf_ref) = refs
    assert conv_w_len == 3
    xb = x_ref[0].astype(BF16)

    def proj(g):
        return jnp.dot(xb, w_ref[:, g * d:(g + 1) * d], preferred_element_type=F32)

    u = proj(2) * proj(0)
    if shifted_inputs:
        um2 = um2_ref[0]
        um1 = um1_ref[0]
        u_ref[0] = u
    else:
        @pl.when(pl.program_id(1) == 0)
        def _():
            ubuf_ref[0:SUBLANE, :] = jnp.zeros((SUBLANE, d), F32)
        ubuf_ref[SUBLANE:SUBLANE + tb, :] = u
        um1 = ubuf_ref[SUBLANE - 1:SUBLANE - 1 + tb, :]
        um2 = ubuf_ref[SUBLANE - 2:SUBLANE - 2 + tb, :]
        ubuf_ref[0:SUBLANE, :] = u[tb - SUBLANE:tb, :]
        nc_ref[0] = u[tb - 2:tb, :]
    conv = um2 * cw_ref[0:1, :] + um1 * cw_ref[1:2, :] + u * cw_ref[2:3, :]
    y_conv = jnp.dot((proj(1) * conv).astype(BF16), wco_ref[...], preferred_element_type=F32)
    yc_ref[0] = _sigmoid(proj(6)) * y_conv
    ga_ref[0] = proj(7)
    qb_ref[0] = (proj(3) * q_scale).astype(BF16)
    k = proj(4)
    k_ref[0] = k
    kb_ref[0] = k.astype(BF16)
    v = proj(5)
    v_ref[0] = v
    vb_ref[0] = v.astype(BF16)


def _inproj_call(x, w_in_b, conv_w, w_conv_out_b, state=None, *, tb):
    bsz, t, d = x.shape
    assert w_in_b.shape == (d, 8 * d)
    q_scale = HEAD_DIM ** -0.5 * LOG2E
    shifted = state is not None
    if shifted:
        assert t == 1
        xr = x.reshape(1, bsz, d)
        rows, nb, nt = bsz, 1, 1
        tb = bsz
    else:
        xr = x
        rows, nb, nt = t, bsz, t // tb
        assert t % tb == 0 and tb % SUBLANE == 0
    blk = lambda: pl.BlockSpec((1, tb, d), lambda b, i: (b, i, 0))
    in_specs = [blk(), _resident((d, 8 * d)), _resident(conv_w.shape), _resident((d, d))]
    args = [xr, w_in_b, conv_w, w_conv_out_b]
    f32_out = jax.ShapeDtypeStruct((nb, rows, d), F32)
    bf_out = jax.ShapeDtypeStruct((nb, rows, d), BF16)
    out_shape = [f32_out, f32_out, bf_out, bf_out, bf_out, f32_out, f32_out]
    out_specs = [blk() for _ in range(7)]
    scratch = []
    if shifted:
        in_specs += [blk(), blk()]
        args += [state[:, 0, :].reshape(1, bsz, d), state[:, 1, :].reshape(1, bsz, d)]
        out_shape.append(f32_out)
        out_specs.append(blk())
    else:
        out_shape.append(jax.ShapeDtypeStruct((nb, 2, d), F32))
        out_specs.append(pl.BlockSpec((1, 2, d), lambda b, i: (b, 0, 0)))
        scratch.append(pltpu.VMEM((tb + SUBLANE, d), F32))
    body = functools.partial(_inproj_body, d=d, tb=tb, conv_w_len=conv_w.shape[0],
                             shifted_inputs=shifted, q_scale=q_scale)
    return pl.pallas_call(
        body, grid=(nb, nt), in_specs=in_specs, out_specs=out_specs, out_shape=out_shape,
        scratch_shapes=scratch, compiler_params=_cparams(2), name="inproj_conv")(*args)


def _t5_bucket(n, num_buckets):
    n = jnp.maximum(n, 0)
    max_exact = num_buckets // 2
    nf = jnp.maximum(n, 1).astype(F32)
    large = max_exact + (jnp.log(nf / max_exact) / math.log(MAX_DISTANCE / max_exact)
                         * (num_buckets - max_exact)).astype(jnp.int32)
    large = jnp.minimum(large, num_buckets - 1)
    return jnp.where(n < max_exact, n, large)


def _bias_by_distance(rel_bias, dist):
    nbk = rel_bias.shape[0]
    rb = (rel_bias.astype(F32) - rel_bias.astype(F32)[nbk - 1][None, :]) * LOG2E
    onehot = (_t5_bucket(dist, nbk)[..., None] == jnp.arange(nbk, dtype=jnp.int32)).astype(F32)
    return jnp.moveaxis(jnp.dot(onehot, rb, precision=lax.Precision.HIGHEST), -1, 0)


def _attn_prompt_body(q_ref, k_ref, v_ref, bias_ref, g_ref, lq1_ref, lk1_ref, lq2_ref, lk2_ref,
                      o_ref, qq_ref, m_ref, l_ref, acc_ref, *, tq, rsub, lam_init):
    qi = pl.program_id(2)
    rows = 2 * tq
    nlt = tq // LANE
    q = q_ref[0]
    lane = lax.broadcasted_iota(jnp.int32, q.shape, 1)
    zero = jnp.zeros_like(q)
    qq_ref[0:tq, :] = jnp.where(lane < HEAD_DIM, q, zero)
    qq_ref[tq:rows, :] = jnp.where(lane >= HEAD_DIM, q, zero)

    def key_tiles(s, r0, chunks_back):
        if chunks_back is None:
            return [s[:, j * LANE:(j + 1) * LANE] for j in range(nlt)]
        t0, t1 = bias_ref[0, 0], bias_ref[0, 1]
        neg = jnp.full((LANE, LANE), NEG_BIG, F32)
        tiles = []
        for j in range(nlt):
            blocks = []
            for rb in range(rsub // LANE):
                off = (r0 % tq) // LANE + rb - j + chunks_back * nlt
                blk = s[rb * LANE:(rb + 1) * LANE, j * LANE:(j + 1) * LANE]
                blocks.append(neg if off < 0 else blk + t0 if off == 0
                              else blk + t1 if off == 1 else blk)
            tiles.append(jnp.concatenate(blocks, axis=0))
        return tiles

    def chunk(ki, chunks_back, first):
        start = pl.multiple_of(ki * tq, tq)
        kc = k_ref[0, pl.ds(start, tq), :]
        vc = v_ref[0, pl.ds(start, tq), :]
        for r0 in range(0, rows, rsub):
            rs = slice(r0, r0 + rsub)
            s = lax.dot_general(qq_ref[rs, :], kc, (((1,), (1,)), ((), ())),
                                preferred_element_type=F32)
            tiles = key_tiles(s, r0, chunks_back)
            mx = jnp.max(functools.reduce(jnp.maximum, tiles), axis=-1, keepdims=True)
            mx = jnp.broadcast_to(mx, (rsub, LANE))
            if first:
                m_new = mx
            else:
                m_prev = m_ref[rs, :]
                m_new = jnp.maximum(m_prev, mx)
                alpha = jnp.exp2(m_prev - m_new)
            ps = [jnp.exp2(t - m_new) for t in tiles]
            psum = functools.reduce(lambda a, b: a + b, ps)
            pv = jnp.dot(jnp.concatenate(ps, axis=1).astype(BF16), vc,
                         preferred_element_type=F32)
            if first:
                l_ref[rs, :] = psum
                acc_ref[rs, :] = pv
            else:
                l_ref[rs, :] = alpha * l_ref[rs, :] + psum
                acc_ref[rs, :] = alpha * acc_ref[rs, :] + pv
            m_ref[rs, :] = m_new

    chunk(qi, 0, True)

    @pl.when(qi > 0)
    def _():
        chunk(qi - 1, 1, False)

    def far(ki, carry):
        chunk(ki, None, False)
        return carry

    lax.fori_loop(0, jnp.maximum(qi - 1, 0), far, 0)

    lam = _lambda(lq1_ref[...], lk1_ref[...], lq2_ref[...], lk2_ref[...], lam_init)
    on = acc_ref[...] / jnp.sum(l_ref[...], axis=-1, keepdims=True)
    o = on[:tq] - lam * on[tq:]
    o = o * lax.rsqrt(jnp.mean(o * o, axis=-1, keepdims=True) + RMS_EPS)
    o_ref[0] = (o * g_ref[...] * (1.0 - lam_init)).astype(BF16)


def _attn_prompt_call(qb, kb, vb, rel_bias, subln_g, lams, lam_init, *, tq, rsub):
    bsz, t, d = qb.shape
    hw = 2 * HEAD_DIM
    nh = d // hw
    assert hw == LANE and t % tq == 0 and tq >= MAX_DISTANCE and tq % rsub == 0
    assert MAX_DISTANCE <= LANE
    i = jnp.arange(LANE, dtype=jnp.int32)
    d0 = i[:, None] - i[None, :]
    blk0 = jnp.where((d0 >= 0)[None], _bias_by_distance(rel_bias, d0), NEG_BIG)
    bias = jnp.stack([blk0, _bias_by_distance(rel_bias, d0 + LANE)], axis=1)
    row = lambda: pl.BlockSpec((1, HEAD_DIM), lambda b, h, qi: (0, 0))
    body = functools.partial(_attn_prompt_body, tq=tq, rsub=rsub, lam_init=lam_init)
    stat = lambda: pltpu.VMEM((2 * tq, hw), F32)
    return pl.pallas_call(
        body, grid=(bsz, nh, t // tq),
        in_specs=[pl.BlockSpec((1, tq, hw), lambda b, h, qi: (b, qi, h)),
                  pl.BlockSpec((1, t, hw), lambda b, h, qi: (b, 0, h)),
                  pl.BlockSpec((1, t, hw), lambda b, h, qi: (b, 0, h)),
                  pl.BlockSpec((1, 2, LANE, LANE), lambda b, h, qi: (h, 0, 0, 0)),
                  pl.BlockSpec((1, hw), lambda b, h, qi: (0, 0)),
                  row(), row(), row(), row()],
        out_specs=pl.BlockSpec((1, tq, hw), lambda b, h, qi: (b, qi, h)),
        out_shape=jax.ShapeDtypeStruct((bsz, t, d), BF16),
        scratch_shapes=[pltpu.VMEM((2 * tq, hw), BF16), stat(), stat(), stat()],
        compiler_params=_cparams(3), name="attn_prompt")(
            qb, kb, vb, bias, subln_g.reshape(1, hw).astype(F32), *lams)


def _attn_sample_body(pt_ref, q_ref, kn_ref, vn_ref, bias_ref, g_ref, expand_ref,
                      lq1_ref, lk1_ref, lq2_ref, lk2_ref, *rest, pages_per_step, nh, lam_init):
    k_refs = rest[:pages_per_step]
    v_refs = rest[pages_per_step:2 * pages_per_step]
    o_ref, m_ref, l_ref, acc_ref = rest[2 * pages_per_step:]
    del pt_ref
    step = pl.program_id(1)
    last = pl.num_programs(1) - 1
    d = q_ref.shape[-1]
    nr = 2 * nh
    page = k_refs[0].shape[2]

    rid = lax.broadcasted_iota(jnp.int32, (nr, d), 0)
    lid = lax.broadcasted_iota(jnp.int32, (nr, d), 1)
    sel = (lid // HEAD_DIM) == (2 * (rid % nh) + rid // nh)
    qrows_b = jnp.where(sel, jnp.broadcast_to(q_ref[0], (nr, d)), 0.0).astype(BF16)

    @pl.when(step == 0)
    def _():
        m_ref[...] = jnp.full(m_ref.shape, NEG_BIG, F32)
        l_ref[...] = jnp.zeros(l_ref.shape, F32)
        acc_ref[...] = jnp.zeros(acc_ref.shape, F32)

    tiles = [jnp.dot(qrows_b, k_refs[j][0].astype(BF16), preferred_element_type=F32)
             for j in range(pages_per_step)]
    tiles[-1] = tiles[-1] + jnp.where(step == last, bias_ref[:, 0:page], 0.0)
    mx = jnp.max(functools.reduce(jnp.maximum, tiles), axis=-1, keepdims=True)
    m_prev = m_ref[...]
    m_new = jnp.maximum(m_prev, mx)
    alpha = jnp.exp2(m_prev - m_new)
    ps = [jnp.exp2(t - m_new) for t in tiles]
    psum = jnp.sum(functools.reduce(lambda a, b: a + b, ps), axis=-1, keepdims=True)
    pe = jnp.dot(jnp.concatenate(ps, axis=0).astype(BF16), expand_ref[...],
                 preferred_element_type=F32)
    rid2 = lax.broadcasted_iota(jnp.int32, pe.shape, 0)
    lid2 = lax.broadcasted_iota(jnp.int32, pe.shape, 1)
    pe = jnp.where((lid2 % nh) == (rid2 % nh), pe, 0.0).astype(BF16)
    pv = alpha * acc_ref[...]
    for j in range(pages_per_step):
        pv = jnp.dot(pe[j * nr:(j + 1) * nr], v_refs[j][0].astype(BF16),
                     preferred_element_type=F32) + pv
    l_new = alpha * l_ref[...] + psum
    m_ref[...] = m_new
    l_ref[...] = l_new
    acc_ref[...] = pv

    @pl.when(step == last)
    def _():
        s_new = jnp.sum(qrows_b.astype(F32) * kn_ref[0], axis=-1, keepdims=True)
        s_new = s_new + bias_ref[:, page:page + 1]
        m_fin = jnp.maximum(m_new, s_new)
        a_fin = jnp.exp2(m_new - m_fin)
        p_new = jnp.exp2(s_new - m_fin)
        vn = vn_ref[0]
        acc = a_fin * pv + p_new * jnp.concatenate([vn, vn], axis=0)
        on = acc / (a_fin * l_new + p_new)
        lam = _lambda(lq1_ref[...], lk1_ref[...], lq2_ref[...], lk2_ref[...], lam_init)
        o = on[:nh] - lam * on[nh:]
        o = o * lax.rsqrt(jnp.mean(o * o, axis=-1, keepdims=True) + RMS_EPS)
        o_ref[0] = (o * g_ref[...] * (1.0 - lam_init)).astype(BF16)


def _attn_sample_call(q, k_new, v_new, cache_kt, cache_v2, page_table, rel_bias, subln_g,
                      lams, lam_init, *, pages_per_step):
    s_cnt, _, d = q.shape
    n_pages = page_table.shape[1]
    page = cache_kt.shape[2]
    hw = 2 * HEAD_DIM
    nh = d // hw
    assert n_pages % pages_per_step == 0 and page >= MAX_DISTANCE and hw == LANE
    dist = jnp.concatenate([page - jnp.arange(page, dtype=jnp.int32),
                            jnp.zeros((LANE,), jnp.int32)])
    bias = jnp.tile(_bias_by_distance(rel_bias, dist), (2, 1))
    expand = jnp.repeat(jnp.eye(page, dtype=BF16), nh, axis=1)
    pt_flat = page_table.reshape(-1).astype(jnp.int32)

    def page_spec(j, shape):
        return pl.BlockSpec(
            (1,) + shape,
            lambda s, p, pt: (pt[s * n_pages + p * pages_per_step + j], 0, 0))

    row = lambda: pl.BlockSpec((1, HEAD_DIM), lambda s, p, pt: (0, 0))
    tok = lambda: pl.BlockSpec((1, 1, d), lambda s, p, pt: (s, 0, 0))
    grid_spec = pltpu.PrefetchScalarGridSpec(
        num_scalar_prefetch=1, grid=(s_cnt, n_pages // pages_per_step),
        in_specs=[tok(), tok(), pl.BlockSpec((1, nh, hw), lambda s, p, pt: (s, 0, 0)),
                  pl.BlockSpec(bias.shape, lambda s, p, pt: (0, 0)),
                  pl.BlockSpec((1, hw), lambda s, p, pt: (0, 0)),
                  pl.BlockSpec(expand.shape, lambda s, p, pt: (0, 0)),
                  row(), row(), row(), row()]
                 + [page_spec(j, (d, page)) for j in range(pages_per_step)]
                 + [page_spec(j, (page * nh, hw)) for j in range(pages_per_step)],
        out_specs=pl.BlockSpec((1, nh, hw), lambda s, p, pt: (s, 0, 0)),
        scratch_shapes=[pltpu.VMEM((2 * nh, 1), F32), pltpu.VMEM((2 * nh, 1), F32),
                        pltpu.VMEM((2 * nh, hw), F32)])
    body = functools.partial(_attn_sample_body, pages_per_step=pages_per_step, nh=nh,
                             lam_init=lam_init)
    o = pl.pallas_call(
        body, grid_spec=grid_spec, out_shape=jax.ShapeDtypeStruct((s_cnt, nh, hw), BF16),
        compiler_params=_cparams(2), name="attn_sample")(
            pt_flat, q, k_new, v_new, bias, subln_g.reshape(1, hw).astype(F32), expand, *lams,
            *([cache_kt] * pages_per_step), *([cache_v2] * pages_per_step))
    return o.reshape(s_cnt, d)


def _merge_body(o_ref, yc_ref, ga_ref, x_ref, wao_ref, wo_ref, g_ref, b_ref, x1_ref, *, alpha):
    y_attn = jnp.dot(o_ref[...], wao_ref[...], preferred_element_type=F32)
    merged = yc_ref[...] + _sigmoid(ga_ref[...]) * y_attn
    mix = jnp.dot(merged.astype(BF16), wo_ref[...], preferred_element_type=F32)
    x1_ref[...] = _layer_norm(alpha * x_ref[...] + mix, g_ref[...], b_ref[...])


def _merge_call(o_b, yc, ga, x, w_attn_out_b, w_o_b, ln_g, ln_b, alpha, *, tb):
    n, d = x.shape
    tb = min(tb, n)
    assert n % tb == 0
    blk = lambda: pl.BlockSpec((tb, d), lambda i: (i, 0))
    return pl.pallas_call(
        functools.partial(_merge_body, alpha=alpha), grid=(n // tb,),
        in_specs=[blk(), blk(), blk(), blk(), _resident((d, d)), _resident((d, d)),
                  _resident((1, d)), _resident((1, d))],
        out_specs=blk(), out_shape=jax.ShapeDtypeStruct((n, d), F32),
        compiler_params=_cparams(1), name="merge_ln")(
            o_b, yc, ga, x, w_attn_out_b, w_o_b, ln_g.reshape(1, d), ln_b.reshape(1, d))


def _gelu_tanh(x):
    c = math.sqrt(2.0 / math.pi)
    return 0.5 * x * (1.0 + jnp.tanh(c * (x + 0.044715 * (x * x * x))))


def _top_values(s, n):
    vals = []
    cur = s
    for r in range(n):
        m = jnp.max(cur, axis=0, keepdims=True)
        vals.append(m)
        if r + 1 < n:
            cur = jnp.where(cur >= m, -jnp.inf, cur)
    return vals


def _peer_body(x1_ref, p_ref, wqt_ref, keys_ref, u_ref, vt_ref, g_ref, b_ref, pg_ref, pp_ref,
               out_ref, xb_ref, th_ref, al_ref, s2_ref, be_ref, ta_ref, tb_ref, yt_ref, *wt_refs,
               nh, nk, dkh, alpha, rows_per_chunk):
    e = pl.program_id(1)
    tb = x1_ref.shape[0]
    n_top = PEER_TOPK + 1

    @pl.when(e == 0)
    def _():
        xb = x1_ref[...].astype(BF16)
        xb_ref[...] = xb
        qt = lax.dot_general(wqt_ref[...], xb, (((1,), (1,)), ((), ())),
                             preferred_element_type=F32)
        for h in range(nh):
            base = h * 2 * dkh
            s1 = jnp.dot(keys_ref[h, 0], qt[base:base + dkh], precision=lax.Precision.HIGHEST,
                         preferred_element_type=F32)
            s2 = jnp.dot(keys_ref[h, 1], qt[base + dkh:base + 2 * dkh],
                         precision=lax.Precision.HIGHEST, preferred_element_type=F32)
            th_ref[h] = s1
            s2_ref[h] = s2
            for r, (a, b) in enumerate(zip(_top_values(s1, n_top), _top_values(s2, n_top))):
                ta_ref[r, h:h + 1, :] = a
                tb_ref[r, h:h + 1, :] = b
        cands = [ta_ref[k] + tb_ref[l] for k in range(n_top) for l in range(n_top)
                 if (k + 1) * (l + 1) <= n_top]
        best = []
        for r in range(n_top):
            m = functools.reduce(jnp.maximum, cands)
            best.append(m)
            if r + 1 < n_top:
                cands = [jnp.where(c >= m, -jnp.inf, c) for c in cands]
        z = functools.reduce(lambda a, b: a + b,
                             [jnp.exp(c - best[0]) for c in best[:PEER_TOPK]])
        tau = 0.5 * (best[PEER_TOPK - 1] + best[PEER_TOPK])
        inv_z = 1.0 / z
        a1 = ta_ref[0]
        b1 = tb_ref[0]
        for h in range(nh):
            s1 = th_ref[h]
            al_ref[h] = jnp.exp(s1 - a1[h:h + 1]) * inv_z[h:h + 1]
            th_ref[h] = tau[h:h + 1] - s1
            be_ref[h] = jnp.exp(s2_ref[h] - b1[h:h + 1])
        yt_ref[...] = jnp.zeros(yt_ref.shape, F32)

    def expert_acts(part):
        r0 = part * SUBLANE * nk
        return lax.dot_general(u_ref[r0:r0 + SUBLANE * nk, :], xb_ref[...],
                               (((1,), (1,)), ((), ())), preferred_element_type=F32)

    n_parts = rows_per_chunk // SUBLANE
    yt = yt_ref[...]
    at_next = expert_acts(0)
    for part in range(n_parts):
        at = at_next
        if part + 1 < n_parts:
            at_next = expert_acts(part + 1)
        i0 = pl.multiple_of(e * rows_per_chunk + part * SUBLANE, SUBLANE)
        wt_ref = wt_refs[part]
        for il in range(SUBLANE):
            gsum = jnp.zeros((nk, tb), F32)
            for h in range(nh):
                th = th_ref[h, pl.ds(i0, SUBLANE), :][il:il + 1]
                al = al_ref[h, pl.ds(i0, SUBLANE), :][il:il + 1]
                gsum = gsum + jnp.where(s2_ref[h] >= th, be_ref[h], 0.0) * al
            act = _gelu_tanh(at[il * nk:(il + 1) * nk])
            wt_ref[il * nk:(il + 1) * nk, :] = (gsum * act).astype(BF16)
        r0 = part * SUBLANE * nk
        yt = jnp.dot(vt_ref[:, r0:r0 + SUBLANE * nk], wt_ref[...],
                     preferred_element_type=F32) + yt
    yt_ref[...] = yt

    @pl.when(e == pl.num_programs(1) - 1)
    def _():
        x1 = x1_ref[...]
        x2 = _layer_norm(alpha * x1 + yt_ref[...].T, g_ref[...], b_ref[...])
        gate = _sigmoid(jnp.dot(x2.astype(BF16), pg_ref[...], preferred_element_type=F32))
        emb = jnp.dot(p_ref[...].astype(BF16), pp_ref[...], preferred_element_type=F32)
        out_ref[...] = x2 + gate * emb


def _peer_call(x1, p, wq_t_b, keys, u_b, v_t_b, ln_g, ln_b, ple_gate_b, ple_proj_b, alpha,
               *, tb, rows_per_chunk):
    n, d = x1.shape
    nh, _, nk, dkh = keys.shape
    n_exp = u_b.shape[0]
    assert n_exp == nk * nk and nk % rows_per_chunk == 0 and rows_per_chunk % SUBLANE == 0
    tb = min(tb, n)
    assert n % tb == 0 and tb % LANE == 0
    ec = rows_per_chunk * nk
    dp = p.shape[1]
    n_top = PEER_TOPK + 1
    body = functools.partial(_peer_body, nh=nh, nk=nk, dkh=dkh, alpha=alpha,
                             rows_per_chunk=rows_per_chunk)
    per_head = lambda: pltpu.VMEM((nh, nk, tb), F32)
    return pl.pallas_call(
        body, grid=(n // tb, n_exp // ec),
        in_specs=[pl.BlockSpec((tb, d), lambda i, e: (i, 0)),
                  pl.BlockSpec((tb, dp), lambda i, e: (i, 0)),
                  _resident(wq_t_b.shape), _resident(keys.shape),
                  pl.BlockSpec((ec, d), lambda i, e: (e, 0)),
                  pl.BlockSpec((d, ec), lambda i, e: (0, e)),
                  _resident((1, d)), _resident((1, d)),
                  _resident(ple_gate_b.shape), _resident(ple_proj_b.shape)],
        out_specs=pl.BlockSpec((tb, d), lambda i, e: (i, 0)),
        out_shape=jax.ShapeDtypeStruct((n, d), F32),
        scratch_shapes=[pltpu.VMEM((tb, d), BF16), per_head(), per_head(), per_head(),
                        per_head(), pltpu.VMEM((n_top, nh, tb), F32),
                        pltpu.VMEM((n_top, nh, tb), F32), pltpu.VMEM((d, tb), F32)]
                       + [pltpu.VMEM((SUBLANE * nk, tb), BF16)] * (rows_per_chunk // SUBLANE),
        compiler_params=_cparams(2), name="peer_ln_ple")(
            x1, p, wq_t_b, keys.astype(F32), u_b, v_t_b, ln_g.reshape(1, d), ln_b.reshape(1, d),
            ple_gate_b, ple_proj_b)


def _tile_sizes(seq):
    pick = lambda pref: max(t for t in (pref, 512, 256, 128) if t <= pref and seq % t == 0)
    return dict(inproj_tb=pick(256), attn_tq=pick(1024), attn_rsub=1024, merge_tb=pick(512),
                peer_tb=pick(512), peer_rows_per_chunk=2 * SUBLANE, sample_pages_per_step=8)


def kernel(x_prompt, x_sample, p_prompt, p_sample, cache_k, cache_v, state_conv, page_table,
           rel_bias, w_in, conv_w, w_conv_out, w_attn_out, w_o, lambda_q1, lambda_k1,
           lambda_q2, lambda_k2, subln_g, ln1_g, ln1_b, ln2_g, ln2_b, peer_wq, peer_keys,
           peer_u, peer_v, ple_proj, ple_gate):
    depth = w_in.shape[0]
    bsz, seq, d = x_prompt.shape
    s_cnt, dec_seq, _ = x_sample.shape
    assert dec_seq == 1
    nh = d // (2 * HEAD_DIM)
    alpha = (2.0 * depth) ** 0.25
    tiles = _tile_sizes(seq)
    n_phys, page = cache_k.shape[1], cache_k.shape[2]

    y_p, y_s = x_prompt, x_sample
    outs = [[] for _ in range(6)]
    for i in range(depth):
        lam_init = 0.8 - 0.6 * math.exp(-0.3 * i)
        lams = [a[i].reshape(1, HEAD_DIM).astype(F32)
                for a in (lambda_q1, lambda_k1, lambda_q2, lambda_k2)]
        w_in_b = w_in[i].astype(BF16)
        wco_b = w_conv_out[i].astype(BF16)
        wao_b = w_attn_out[i].astype(BF16)
        wo_b = w_o[i].astype(BF16)
        wq_t_b = peer_wq[i].T.astype(BF16)
        u_b = peer_u[i].astype(BF16)
        v_t_b = peer_v[i].T.astype(BF16)
        pg_b = ple_gate[i].astype(BF16)
        pp_b = ple_proj[i].astype(BF16)

        def tail(o_b, yc, ga, x, p):
            n = x.shape[0]
            x1 = _merge_call(o_b, yc.reshape(n, d), ga.reshape(n, d), x, wao_b, wo_b,
                             ln1_g[i], ln1_b[i], alpha, tb=tiles["merge_tb"])
            return _peer_call(x1, p, wq_t_b, peer_keys[i], u_b, v_t_b, ln2_g[i], ln2_b[i],
                              pg_b, pp_b, alpha, tb=tiles["peer_tb"],
                              rows_per_chunk=tiles["peer_rows_per_chunk"])

        k_p, v_p, qb, kb, vb, yc, ga, c_p = _inproj_call(
            y_p, w_in_b, conv_w[i], wco_b, tb=tiles["inproj_tb"])
        o_b = _attn_prompt_call(qb, kb, vb, rel_bias, subln_g[i], lams, lam_init,
                                tq=tiles["attn_tq"],
                                rsub=min(tiles["attn_rsub"], tiles["attn_tq"]))
        y_p = tail(o_b.reshape(bsz * seq, d), yc, ga, y_p.reshape(bsz * seq, d),
                   p_prompt[i].reshape(bsz * seq, -1)).reshape(bsz, seq, d)

        k_s, v_s, qb, _, _, yc, ga, u_s = _inproj_call(
            y_s, w_in_b, conv_w[i], wco_b, state=state_conv[i], tb=s_cnt)
        k_s = k_s.reshape(s_cnt, 1, d)
        v_s = v_s.reshape(s_cnt, 1, d)
        cache_kt = jnp.transpose(cache_k[i], (0, 2, 3, 4, 1)).reshape(n_phys, d, page)
        cache_v2 = cache_v[i].reshape(n_phys, page * nh, 2 * HEAD_DIM)
        n_pages = page_table.shape[1]
        pps = max(p for p in range(1, tiles["sample_pages_per_step"] + 1) if n_pages % p == 0)
        o_b = _attn_sample_call(
            qb.reshape(s_cnt, 1, d).astype(F32), k_s, v_s.reshape(s_cnt, nh, 2 * HEAD_DIM),
            cache_kt, cache_v2, page_table, rel_bias, subln_g[i], lams, lam_init,
            pages_per_step=pps)
        c_s = jnp.stack([state_conv[i][:, 1, :], u_s.reshape(s_cnt, d)], axis=1)
        y_s = tail(o_b, yc, ga, y_s.reshape(s_cnt, d),
                   p_sample[i].reshape(s_cnt, -1)).reshape(s_cnt, 1, d)

        outs[0].append(k_p.reshape(bsz, seq, nh, 2, HEAD_DIM))
        outs[1].append(v_p.reshape(bsz, seq, nh, 2 * HEAD_DIM))
        outs[2].append(c_p)
        outs[3].append(k_s.reshape(s_cnt, 1, nh, 2, HEAD_DIM))
        outs[4].append(v_s.reshape(s_cnt, 1, nh, 2 * HEAD_DIM))
        outs[5].append(c_s)
    return (y_p, y_s) + tuple(jnp.stack(o) for o in outs)
```

```python
import functools
import math

import jax
import jax.numpy as jnp
from jax import lax
from jax.experimental import pallas as pl
from jax.experimental.pallas import tpu as pltpu

F32 = jnp.float32
BF16 = jnp.bfloat16

HEAD_DIM = 64
MAX_DISTANCE = 128
PEER_TOPK = 16
LN_EPS = 1e-5
RMS_EPS = 1e-5
NEG_BIG = -1e30
LOG2E = 1.4426950408889634
LANE = 128
SUBLANE = 8
VMEM_LIMIT_BYTES = 56 * 1024 * 1024


def _cparams(n_axes):
    return pltpu.CompilerParams(dimension_semantics=("arbitrary",) * n_axes,
                                vmem_limit_bytes=VMEM_LIMIT_BYTES)


def _resident(shape):
    zeros = (0,) * len(shape)
    return pl.BlockSpec(shape, lambda *_: zeros, pipeline_mode=pl.Buffered(1))


def _sigmoid(x):
    return 1.0 / (1.0 + jnp.exp(-x))


def _layer_norm(x, g, b):
    mu = jnp.mean(x, axis=-1, keepdims=True)
    xc = x - mu
    var = jnp.mean(xc * xc, axis=-1, keepdims=True)
    return xc * lax.rsqrt(var + LN_EPS) * g + b


def _lambda(lq1, lk1, lq2, lk2, lam_init):
    s1 = jnp.sum(lq1 * lk1, axis=-1, keepdims=True)
    s2 = jnp.sum(lq2 * lk2, axis=-1, keepdims=True)
    return jnp.exp(s1) - jnp.exp(s2) + lam_init


def _inproj_body(*refs, d, tb, conv_w_len, shifted_inputs, q_scale):
    if shifted_inputs:
        (x_ref, w_ref, cw_ref, wco_ref, um2_ref, um1_ref,
         k_ref, v_ref, qb_ref, kb_ref, vb_ref, yc_ref, ga_ref, u_ref) = refs
    else:
        (x_ref, w_ref, cw_ref, wco_ref,
         k_ref, v_ref, qb_ref, kb_ref, vb_ref, yc_ref, ga_ref, nc_ref, ubuf_ref) = refs
    assert conv_w_len == 3
    xb = x_ref[0].astype(BF16)

    def proj(g):
        return jnp.dot(xb, w_ref[:, g * d:(g + 1) * d], preferred_element_type=F32)

    u = proj(2) * proj(0)
    if shifted_inputs:
        um2 = um2_ref[0]
        um1 = um1_ref[0]
        u_ref[0] = u
    else:
        @pl.when(pl.program_id(1) == 0)
        def _():
            ubuf_ref[0:SUBLANE, :] = jnp.zeros((SUBLANE, d), F32)
        ubuf_ref[SUBLANE:SUBLANE + tb, :] = u
        um1 = ubuf_ref[SUBLANE - 1:SUBLANE - 1 + tb, :]
        um2 = ubuf_ref[SUBLANE - 2:SUBLANE - 2 + tb, :]
        ubuf_ref[0:SUBLANE, :] = u[tb - SUBLANE:tb, :]
        nc_ref[0] = u[tb - 2:tb, :]
    conv = um2 * cw_ref[0:1, :] + um1 * cw_ref[1:2, :] + u * cw_ref[2:3, :]
    y_conv = jnp.dot((proj(1) * conv).astype(BF16), wco_ref[...], preferred_element_type=F32)
    yc_ref[0] = _sigmoid(proj(6)) * y_conv
    ga_ref[0] = proj(7)
    qb_ref[0] = (proj(3) * q_scale).astype(BF16)
    k = proj(4)
    k_ref[0] = k
    kb_ref[0] = k.astype(BF16)
    v = proj(5)
    v_ref[0] = v
    vb_ref[0] = v.astype(BF16)


def _inproj_call(x, w_in_b, conv_w, w_conv_out_b, state=None, *, tb):
    bsz, t, d = x.shape
    assert w_in_b.shape == (d, 8 * d)
    q_scale = HEAD_DIM ** -0.5 * LOG2E
    shifted = state is not None
    if shifted:
        assert t == 1
        xr = x.reshape(1, bsz, d)
        rows, nb, nt = bsz, 1, 1
        tb = bsz
    else:
        xr = x
        rows, nb, nt = t, bsz, t // tb
        assert t % tb == 0 and tb % SUBLANE == 0
    blk = lambda: pl.BlockSpec((1, tb, d), lambda b, i: (b, i, 0))
    in_specs = [blk(), _resident((d, 8 * d)), _resident(conv_w.shape), _resident((d, d))]
    args = [xr, w_in_b, conv_w, w_conv_out_b]
    f32_out = jax.ShapeDtypeStruct((nb, rows, d), F32)
    bf_out = jax.ShapeDtypeStruct((nb, rows, d), BF16)
    out_shape = [f32_out, f32_out, bf_out, bf_out, bf_out, f32_out, f32_out]
    out_specs = [blk() for _ in range(7)]
    scratch = []
    if shifted:
        in_specs += [blk(), blk()]
        args += [state[:, 0, :].reshape(1, bsz, d), state[:, 1, :].reshape(1, bsz, d)]
        out_shape.append(f32_out)
        out_specs.append(blk())
    else:
        out_shape.append(jax.ShapeDtypeStruct((nb, 2, d), F32))
        out_specs.append(pl.BlockSpec((1, 2, d), lambda b, i: (b, 0, 0)))
        scratch.append(pltpu.VMEM((tb + SUBLANE, d), F32))
    body = functools.partial(_inproj_body, d=d, tb=tb, conv_w_len=conv_w.shape[0],
                             shifted_inputs=shifted, q_scale=q_scale)
    return pl.pallas_call(
        body, grid=(nb, nt), in_specs=in_specs, out_specs=out_specs, out_shape=out_shape,
        scratch_shapes=scratch, compiler_params=_cparams(2), name="inproj_conv")(*args)


def _t5_bucket(n, num_buckets):
    n = jnp.maximum(n, 0)
    max_exact = num_buckets // 2
    nf = jnp.maximum(n, 1).astype(F32)
    large = max_exact + (jnp.log(nf / max_exact) / math.log(MAX_DISTANCE / max_exact)
                         * (num_buckets - max_exact)).astype(jnp.int32)
    large = jnp.minimum(large, num_buckets - 1)
    return jnp.where(n < max_exact, n, large)


def _bias_by_distance(rel_bias, dist):
    nbk = rel_bias.shape[0]
    rb = (rel_bias.astype(F32) - rel_bias.astype(F32)[nbk - 1][None, :]) * LOG2E
    onehot = (_t5_bucket(dist, nbk)[..., None] == jnp.arange(nbk, dtype=jnp.int32)).astype(F32)
    return jnp.moveaxis(jnp.dot(onehot, rb, precision=lax.Precision.HIGHEST), -1, 0)


def _attn_prompt_body(q_ref, k_ref, v_ref, bias_ref, g_ref, lq1_ref, lk1_ref, lq2_ref, lk2_ref,
                      o_ref, qq_ref, m_ref, l_ref, acc_ref, *, tq, rsub, lam_init):
    qi = pl.program_id(2)
    rows = 2 * tq
    nlt = tq // LANE
    q = q_ref[0]
    lane = lax.broadcasted_iota(jnp.int32, q.shape, 1)
    zero = jnp.zeros_like(q)
    qq_ref[0:tq, :] = jnp.where(lane < HEAD_DIM, q, zero)
    qq_ref[tq:rows, :] = jnp.where(lane >= HEAD_DIM, q, zero)

    def key_tiles(s, r0, chunks_back):
        if chunks_back is None:
            return [s[:, j * LANE:(j + 1) * LANE] for j in range(nlt)]
        t0, t1 = bias_ref[0, 0], bias_ref[0, 1]
        neg = jnp.full((LANE, LANE), NEG_BIG, F32)
        tiles = []
        for j in range(nlt):
            blocks = []
            for rb in range(rsub // LANE):
                off = (r0 % tq) // LANE + rb - j + chunks_back * nlt
                blk = s[rb * LANE:(rb + 1) * LANE, j * LANE:(j + 1) * LANE]
                blocks.append(neg if off < 0 else blk + t0 if off == 0
                              else blk + t1 if off == 1 else blk)
            tiles.append(jnp.concatenate(blocks, axis=0))
        return tiles

    def chunk(ki, chunks_back, first):
        start = pl.multiple_of(ki * tq, tq)
        kc = k_ref[0, pl.ds(start, tq), :]
        vc = v_ref[0, pl.ds(start, tq), :]
        for r0 in range(0, rows, rsub):
            rs = slice(r0, r0 + rsub)
            s = lax.dot_general(qq_ref[rs, :], kc, (((1,), (1,)), ((), ())),
                                preferred_element_type=F32)
            tiles = key_tiles(s, r0, chunks_back)
            mx = jnp.max(functools.reduce(jnp.maximum, tiles), axis=-1, keepdims=True)
            mx = jnp.broadcast_to(mx, (rsub, LANE))
            if first:
                m_new = mx
            else:
                m_prev = m_ref[rs, :]
                m_new = jnp.maximum(m_prev, mx)
                alpha = jnp.exp2(m_prev - m_new)
            ps = [jnp.exp2(t - m_new) for t in tiles]
            psum = functools.reduce(lambda a, b: a + b, ps)
            pv = jnp.dot(jnp.concatenate(ps, axis=1).astype(BF16), vc,
                         preferred_element_type=F32)
            if first:
                l_ref[rs, :] = psum
                acc_ref[rs, :] = pv
            else:
                l_ref[rs, :] = alpha * l_ref[rs, :] + psum
                acc_ref[rs, :] = alpha * acc_ref[rs, :] + pv
            m_ref[rs, :] = m_new

    chunk(qi, 0, True)

    @pl.when(qi > 0)
    def _():
        chunk(qi - 1, 1, False)

    def far(ki, carry):
        chunk(ki, None, False)
        return carry

    lax.fori_loop(0, jnp.maximum(qi - 1, 0), far, 0)

    lam = _lambda(lq1_ref[...], lk1_ref[...], lq2_ref[...], lk2_ref[...], lam_init)
    on = acc_ref[...] / jnp.sum(l_ref[...], axis=-1, keepdims=True)
    o = on[:tq] - lam * on[tq:]
    o = o * lax.rsqrt(jnp.mean(o * o, axis=-1, keepdims=True) + RMS_EPS)
    o_ref[0] = (o * g_ref[...] * (1.0 - lam_init)).astype(BF16)


def _attn_prompt_call(qb, kb, vb, rel_bias, subln_g, lams, lam_init, *, tq, rsub):
    bsz, t, d = qb.shape
    hw = 2 * HEAD_DIM
    nh = d // hw
    assert hw == LANE and t % tq == 0 and tq >= MAX_DISTANCE and tq % rsub == 0
    assert MAX_DISTANCE <= LANE
    i = jnp.arange(LANE, dtype=jnp.int32)
    d0 = i[:, None] - i[None, :]
    blk0 = jnp.where((d0 >= 0)[None], _bias_by_distance(rel_bias, d0), NEG_BIG)
    bias = jnp.stack([blk0, _bias_by_distance(rel_bias, d0 + LANE)], axis=1)
    row = lambda: pl.BlockSpec((1, HEAD_DIM), lambda b, h, qi: (0, 0))
    body = functools.partial(_attn_prompt_body, tq=tq, rsub=rsub, lam_init=lam_init)
    stat = lambda: pltpu.VMEM((2 * tq, hw), F32)
    return pl.pallas_call(
        body, grid=(bsz, nh, t // tq),
        in_specs=[pl.BlockSpec((1, tq, hw), lambda b, h, qi: (b, qi, h)),
                  pl.BlockSpec((1, t, hw), lambda b, h, qi: (b, 0, h)),
                  pl.BlockSpec((1, t, hw), lambda b, h, qi: (b, 0, h)),
                  pl.BlockSpec((1, 2, LANE, LANE), lambda b, h, qi: (h, 0, 0, 0)),
                  pl.BlockSpec((1, hw), lambda b, h, qi: (0, 0)),
                  row(), row(), row(), row()],
        out_specs=pl.BlockSpec((1, tq, hw), lambda b, h, qi: (b, qi, h)),
        out_shape=jax.ShapeDtypeStruct((bsz, t, d), BF16),
        scratch_shapes=[pltpu.VMEM((2 * tq, hw), BF16), stat(), stat(), stat()],
        compiler_params=_cparams(3), name="attn_prompt")(
            qb, kb, vb, bias, subln_g.reshape(1, hw).astype(F32), *lams)


def _attn_sample_body(pt_ref, q_ref, kn_ref, vn_ref, bias_ref, g_ref, expand_ref,
                      lq1_ref, lk1_ref, lq2_ref, lk2_ref, *rest, pages_per_step, nh, lam_init):
    k_refs = rest[:pages_per_step]
    v_refs = rest[pages_per_step:2 * pages_per_step]
    o_ref, m_ref, l_ref, acc_ref = rest[2 * pages_per_step:]
    del pt_ref
    step = pl.program_id(1)
    last = pl.num_programs(1) - 1
    d = q_ref.shape[-1]
    nr = 2 * nh
    page = k_refs[0].shape[2]

    rid = lax.broadcasted_iota(jnp.int32, (nr, d), 0)
    lid = lax.broadcasted_iota(jnp.int32, (nr, d), 1)
    sel = (lid // HEAD_DIM) == (2 * (rid % nh) + rid // nh)
    qrows_b = jnp.where(sel, jnp.broadcast_to(q_ref[0], (nr, d)), 0.0).astype(BF16)

    @pl.when(step == 0)
    def _():
        m_ref[...] = jnp.full(m_ref.shape, NEG_BIG, F32)
        l_ref[...] = jnp.zeros(l_ref.shape, F32)
        acc_ref[...] = jnp.zeros(acc_ref.shape, F32)

    tiles = [jnp.dot(qrows_b, k_refs[j][0].astype(BF16), preferred_element_type=F32)
             for j in range(pages_per_step)]
    tiles[-1] = tiles[-1] + jnp.where(step == last, bias_ref[:, 0:page], 0.0)
    mx = jnp.max(functools.reduce(jnp.maximum, tiles), axis=-1, keepdims=True)
    m_prev = m_ref[...]
    m_new = jnp.maximum(m_prev, mx)
    alpha = jnp.exp2(m_prev - m_new)
    ps = [jnp.exp2(t - m_new) for t in tiles]
    psum = jnp.sum(functools.reduce(lambda a, b: a + b, ps), axis=-1, keepdims=True)
    pe = jnp.dot(jnp.concatenate(ps, axis=0).astype(BF16), expand_ref[...],
                 preferred_element_type=F32)
    rid2 = lax.broadcasted_iota(jnp.int32, pe.shape, 0)
    lid2 = lax.broadcasted_iota(jnp.int32, pe.shape, 1)
    pe = jnp.where((lid2 % nh) == (rid2 % nh), pe, 0.0).astype(BF16)
    pv = alpha * acc_ref[...]
    for j in range(pages_per_step):
        pv = jnp.dot(pe[j * nr:(j + 1) * nr], v_refs[j][0].astype(BF16),
                     preferred_element_type=F32) + pv
    l_new = alpha * l_ref[...] + psum
    m_ref[...] = m_new
    l_ref[...] = l_new
    acc_ref[...] = pv

    @pl.when(step == last)
    def _():
        s_new = jnp.sum(qrows_b.astype(F32) * kn_ref[0], axis=-1, keepdims=True)
        s_new = s_new + bias_ref[:, page:page + 1]
        m_fin = jnp.maximum(m_new, s_new)
        a_fin = jnp.exp2(m_new - m_fin)
        p_new = jnp.exp2(s_new - m_fin)
        vn = vn_ref[0]
        acc = a_fin * pv + p_new * jnp.concatenate([vn, vn], axis=0)
        on = acc / (a_fin * l_new + p_new)
        lam = _lambda(lq1_ref[...], lk1_ref[...], lq2_ref[...], lk2_ref[...], lam_init)
        o = on[:nh] - lam * on[nh:]
        o = o * lax.rsqrt(jnp.mean(o * o, axis=-1, keepdims=True) + RMS_EPS)
        o_ref[0] = (o * g_ref[...] * (1.0 - lam_init)).astype(BF16)


def _attn_sample_call(q, k_new, v_new, cache_kt, cache_v2, page_table, rel_bias, subln_g,
                      lams, lam_init, *, pages_per_step):
    s_cnt, _, d = q.shape
    n_pages = page_table.shape[1]
    page = cache_kt.shape[2]
    hw = 2 * HEAD_DIM
    nh = d // hw
    assert n_pages % pages_per_step == 0 and page >= MAX_DISTANCE and hw == LANE
    dist = jnp.concatenate([page - jnp.arange(page, dtype=jnp.int32),
                            jnp.zeros((LANE,), jnp.int32)])
    bias = jnp.tile(_bias_by_distance(rel_bias, dist), (2, 1))
    expand = jnp.repeat(jnp.eye(page, dtype=BF16), nh, axis=1)
    pt_flat = page_table.reshape(-1).astype(jnp.int32)

    def page_spec(j, shape):
        return pl.BlockSpec(
            (1,) + shape,
            lambda s, p, pt: (pt[s * n_pages + p * pages_per_step + j], 0, 0))

    row = lambda: pl.BlockSpec((1, HEAD_DIM), lambda s, p, pt: (0, 0))
    tok = lambda: pl.BlockSpec((1, 1, d), lambda s, p, pt: (s, 0, 0))
    grid_spec = pltpu.PrefetchScalarGridSpec(
        num_scalar_prefetch=1, grid=(s_cnt, n_pages // pages_per_step),
        in_specs=[tok(), tok(), pl.BlockSpec((1, nh, hw), lambda s, p, pt: (s, 0, 0)),
                  pl.BlockSpec(bias.shape, lambda s, p, pt: (0, 0)),
                  pl.BlockSpec((1, hw), lambda s, p, pt: (0, 0)),
                  pl.BlockSpec(expand.shape, lambda s, p, pt: (0, 0)),
                  row(), row(), row(), row()]
                 + [page_spec(j, (d, page)) for j in range(pages_per_step)]
                 + [page_spec(j, (page * nh, hw)) for j in range(pages_per_step)],
        out_specs=pl.BlockSpec((1, nh, hw), lambda s, p, pt: (s, 0, 0)),
        scratch_shapes=[pltpu.VMEM((2 * nh, 1), F32), pltpu.VMEM((2 * nh, 1), F32),
                        pltpu.VMEM((2 * nh, hw), F32)])
    body = functools.partial(_attn_sample_body, pages_per_step=pages_per_step, nh=nh,
                             lam_init=lam_init)
    o = pl.pallas_call(
        body, grid_spec=grid_spec, out_shape=jax.ShapeDtypeStruct((s_cnt, nh, hw), BF16),
        compiler_params=_cparams(2), name="attn_sample")(
            pt_flat, q, k_new, v_new, bias, subln_g.reshape(1, hw).astype(F32), expand, *lams,
            *([cache_kt] * pages_per_step), *([cache_v2] * pages_per_step))
    return o.reshape(s_cnt, d)


def _merge_body(o_ref, yc_ref, ga_ref, x_ref, wao_ref, wo_ref, g_ref, b_ref, x1_ref, *, alpha):
    y_attn = jnp.dot(o_ref[...], wao_ref[...], preferred_element_type=F32)
    merged = yc_ref[...] + _sigmoid(ga_ref[...]) * y_attn
    mix = jnp.dot(merged.astype(BF16), wo_ref[...], preferred_element_type=F32)
    x1_ref[...] = _layer_norm(alpha * x_ref[...] + mix, g_ref[...], b_ref[...])


def _merge_call(o_b, yc, ga, x, w_attn_out_b, w_o_b, ln_g, ln_b, alpha, *, tb):
    n, d = x.shape
    tb = min(tb, n)
    assert n % tb == 0
    blk = lambda: pl.BlockSpec((tb, d), lambda i: (i, 0))
    return pl.pallas_call(
        functools.partial(_merge_body, alpha=alpha), grid=(n // tb,),
        in_specs=[blk(), blk(), blk(), blk(), _resident((d, d)), _resident((d, d)),
                  _resident((1, d)), _resident((1, d))],
        out_specs=blk(), out_shape=jax.ShapeDtypeStruct((n, d), F32),
        compiler_params=_cparams(1), name="merge_ln")(
            o_b, yc, ga, x, w_attn_out_b, w_o_b, ln_g.reshape(1, d), ln_b.reshape(1, d))


def _gelu_tanh(x):
    c = math.sqrt(2.0 / math.pi)
    return 0.5 * x * (1.0 + jnp.tanh(c * (x + 0.044715 * (x * x * x))))


def _top_values(s, n):
    vals = []
    cur = s
    for r in range(n):
        m = jnp.max(cur, axis=0, keepdims=True)
        vals.append(m)
        if r + 1 < n:
            cur = jnp.where(cur >= m, -jnp.inf, cur)
    return vals


def _peer_body(x1_ref, p_ref, wqt_ref, keys_ref, u_ref, vt_ref, g_ref, b_ref, pg_ref, pp_ref,
               out_ref, xb_ref, th_ref, al_ref, s2_ref, be_ref, ta_ref, tb_ref, yt_ref, *wt_refs,
               nh, nk, dkh, alpha, rows_per_chunk):
    e = pl.program_id(1)
    tb = x1_ref.shape[0]
    n_top = PEER_TOPK + 1

    @pl.when(e == 0)
    def _():
        xb = x1_ref[...].astype(BF16)
        xb_ref[...] = xb
        qt = lax.dot_general(wqt_ref[...], xb, (((1,), (1,)), ((), ())),
                             preferred_element_type=F32)
        for h in range(nh):
            base = h * 2 * dkh
            s1 = jnp.dot(keys_ref[h, 0], qt[base:base + dkh], precision=lax.Precision.HIGHEST,
                         preferred_element_type=F32)
            s2 = jnp.dot(keys_ref[h, 1], qt[base + dkh:base + 2 * dkh],
                         precision=lax.Precision.HIGHEST, preferred_element_type=F32)
            th_ref[h] = s1
            s2_ref[h] = s2
            for r, (a, b) in enumerate(zip(_top_values(s1, n_top), _top_values(s2, n_top))):
                ta_ref[r, h:h + 1, :] = a
                tb_ref[r, h:h + 1, :] = b
        cands = [ta_ref[k] + tb_ref[l] for k in range(n_top) for l in range(n_top)
                 if (k + 1) * (l + 1) <= n_top]
        best = []
        for r in range(n_top):
            m = functools.reduce(jnp.maximum, cands)
            best.append(m)
            if r + 1 < n_top:
                cands = [jnp.where(c >= m, -jnp.inf, c) for c in cands]
        z = functools.reduce(lambda a, b: a + b,
                             [jnp.exp(c - best[0]) for c in best[:PEER_TOPK]])
        tau = 0.5 * (best[PEER_TOPK - 1] + best[PEER_TOPK])
        inv_z = 1.0 / z
        a1 = ta_ref[0]
        b1 = tb_ref[0]
        for h in range(nh):
            s1 = th_ref[h]
            al_ref[h] = jnp.exp(s1 - a1[h:h + 1]) * inv_z[h:h + 1]
            th_ref[h] = tau[h:h + 1] - s1
            be_ref[h] = jnp.exp(s2_ref[h] - b1[h:h + 1])
        yt_ref[...] = jnp.zeros(yt_ref.shape, F32)

    def expert_acts(part):
        r0 = part * SUBLANE * nk
        return lax.dot_general(u_ref[r0:r0 + SUBLANE * nk, :], xb_ref[...],
                               (((1,), (1,)), ((), ())), preferred_element_type=F32)

    n_parts = rows_per_chunk // SUBLANE
    yt = yt_ref[...]
    at_next = expert_acts(0)
    for part in range(n_parts):
        at = at_next
        if part + 1 < n_parts:
            at_next = expert_acts(part + 1)
        i0 = pl.multiple_of(e * rows_per_chunk + part * SUBLANE, SUBLANE)
        wt_ref = wt_refs[part]
        for il in range(SUBLANE):
            gsum = jnp.zeros((nk, tb), F32)
            for h in range(nh):
                th = th_ref[h, pl.ds(i0, SUBLANE), :][il:il + 1]
                al = al_ref[h, pl.ds(i0, SUBLANE), :][il:il + 1]
                gsum = jnp.where(s2_ref[h] >= th, gsum + be_ref[h] * al, gsum)
            act = _gelu_tanh(at[il * nk:(il + 1) * nk])
            wt_ref[il * nk:(il + 1) * nk, :] = (gsum * act).astype(BF16)
        r0 = part * SUBLANE * nk
        yt = jnp.dot(vt_ref[:, r0:r0 + SUBLANE * nk], wt_ref[...],
                     preferred_element_type=F32) + yt
    yt_ref[...] = yt

    @pl.when(e == pl.num_programs(1) - 1)
    def _():
        x1 = x1_ref[...]
        x2 = _layer_norm(alpha * x1 + yt_ref[...].T, g_ref[...], b_ref[...])
        gate = _sigmoid(jnp.dot(x2.astype(BF16), pg_ref[...], preferred_element_type=F32))
        emb = jnp.dot(p_ref[...].astype(BF16), pp_ref[...], preferred_element_type=F32)
        out_ref[...] = x2 + gate * emb


def _peer_call(x1, p, wq_t_b, keys, u_b, v_t_b, ln_g, ln_b, ple_gate_b, ple_proj_b, alpha,
               *, tb, rows_per_chunk):
    n, d = x1.shape
    nh, _, nk, dkh = keys.shape
    n_exp = u_b.shape[0]
    assert n_exp == nk * nk and nk % rows_per_chunk == 0 and rows_per_chunk % SUBLANE == 0
    tb = min(tb, n)
    assert n % tb == 0 and tb % LANE == 0
    ec = rows_per_chunk * nk
    dp = p.shape[1]
    n_top = PEER_TOPK + 1
    body = functools.partial(_peer_body, nh=nh, nk=nk, dkh=dkh, alpha=alpha,
                             rows_per_chunk=rows_per_chunk)
    per_head = lambda: pltpu.VMEM((nh, nk, tb), F32)
    return pl.pallas_call(
        body, grid=(n // tb, n_exp // ec),
        in_specs=[pl.BlockSpec((tb, d), lambda i, e: (i, 0)),
                  pl.BlockSpec((tb, dp), lambda i, e: (i, 0)),
                  _resident(wq_t_b.shape), _resident(keys.shape),
                  pl.BlockSpec((ec, d), lambda i, e: (e, 0)),
                  pl.BlockSpec((d, ec), lambda i, e: (0, e)),
                  _resident((1, d)), _resident((1, d)),
                  _resident(ple_gate_b.shape), _resident(ple_proj_b.shape)],
        out_specs=pl.BlockSpec((tb, d), lambda i, e: (i, 0)),
        out_shape=jax.ShapeDtypeStruct((n, d), F32),
        scratch_shapes=[pltpu.VMEM((tb, d), BF16), per_head(), per_head(), per_head(),
                        per_head(), pltpu.VMEM((n_top, nh, tb), F32),
                        pltpu.VMEM((n_top, nh, tb), F32), pltpu.VMEM((d, tb), F32)]
                       + [pltpu.VMEM((SUBLANE * nk, tb), BF16)] * (rows_per_chunk // SUBLANE),
        compiler_params=_cparams(2), name="peer_ln_ple")(
            x1, p, wq_t_b, keys.astype(F32), u_b, v_t_b, ln_g.reshape(1, d), ln_b.reshape(1, d),
            ple_gate_b, ple_proj_b)


def _tile_sizes(seq):
    pick = lambda pref: max(t for t in (pref, 512, 256, 128) if t <= pref and seq % t == 0)
    return dict(inproj_tb=pick(256), attn_tq=pick(1024), attn_rsub=1024, merge_tb=pick(512),
                peer_tb=pick(512), peer_rows_per_chunk=2 * SUBLANE, sample_pages_per_step=8)


def kernel(x_prompt, x_sample, p_prompt, p_sample, cache_k, cache_v, state_conv, page_table,
           rel_bias, w_in, conv_w, w_conv_out, w_attn_out, w_o, lambda_q1, lambda_k1,
           lambda_q2, lambda_k2, subln_g, ln1_g, ln1_b, ln2_g, ln2_b, peer_wq, peer_keys,
           peer_u, peer_v, ple_proj, ple_gate):
    depth = w_in.shape[0]
    bsz, seq, d = x_prompt.shape
    s_cnt, dec_seq, _ = x_sample.shape
    assert dec_seq == 1
    nh = d // (2 * HEAD_DIM)
    alpha = (2.0 * depth) ** 0.25
    tiles = _tile_sizes(seq)
    n_phys, page = cache_k.shape[1], cache_k.shape[2]

    y_p, y_s = x_prompt, x_sample
    outs = [[] for _ in range(6)]
    for i in range(depth):
        lam_init = 0.8 - 0.6 * math.exp(-0.3 * i)
        lams = [a[i].reshape(1, HEAD_DIM).astype(F32)
                for a in (lambda_q1, lambda_k1, lambda_q2, lambda_k2)]
        w_in_b = w_in[i].astype(BF16)
        wco_b = w_conv_out[i].astype(BF16)
        wao_b = w_attn_out[i].astype(BF16)
        wo_b = w_o[i].astype(BF16)
        wq_t_b = peer_wq[i].T.astype(BF16)
        u_b = peer_u[i].astype(BF16)
        v_t_b = peer_v[i].T.astype(BF16)
        pg_b = ple_gate[i].astype(BF16)
        pp_b = ple_proj[i].astype(BF16)

        def tail(o_b, yc, ga, x, p):
            n = x.shape[0]
            x1 = _merge_call(o_b, yc.reshape(n, d), ga.reshape(n, d), x, wao_b, wo_b,
                             ln1_g[i], ln1_b[i], alpha, tb=tiles["merge_tb"])
            return _peer_call(x1, p, wq_t_b, peer_keys[i], u_b, v_t_b, ln2_g[i], ln2_b[i],
                              pg_b, pp_b, alpha, tb=tiles["peer_tb"],
                              rows_per_chunk=tiles["peer_rows_per_chunk"])

        k_p, v_p, qb, kb, vb, yc, ga, c_p = _inproj_call(
            y_p, w_in_b, conv_w[i], wco_b, tb=tiles["inproj_tb"])
        o_b = _attn_prompt_call(qb, kb, vb, rel_bias, subln_g[i], lams, lam_init,
                                tq=tiles["attn_tq"],
                                rsub=min(tiles["attn_rsub"], tiles["attn_tq"]))
        y_p = tail(o_b.reshape(bsz * seq, d), yc, ga, y_p.reshape(bsz * seq, d),
                   p_prompt[i].reshape(bsz * seq, -1)).reshape(bsz, seq, d)

        k_s, v_s, qb, _, _, yc, ga, u_s = _inproj_call(
            y_s, w_in_b, conv_w[i], wco_b, state=state_conv[i], tb=s_cnt)
        k_s = k_s.reshape(s_cnt, 1, d)
        v_s = v_s.reshape(s_cnt, 1, d)
        cache_kt = jnp.transpose(cache_k[i], (0, 2, 3, 4, 1)).reshape(n_phys, d, page)
        cache_v2 = cache_v[i].reshape(n_phys, page * nh, 2 * HEAD_DIM)
        n_pages = page_table.shape[1]
        pps = max(p for p in range(1, tiles["sample_pages_per_step"] + 1) if n_pages % p == 0)
        o_b = _attn_sample_call(
            qb.reshape(s_cnt, 1, d).astype(F32), k_s, v_s.reshape(s_cnt, nh, 2 * HEAD_DIM),
            cache_kt, cache_v2, page_table, rel_bias, subln_g[i], lams, lam_init,
            pages_per_step=pps)
        c_s = jnp.stack([state_conv[i][:, 1, :], u_s.reshape(s_cnt, d)], axis=1)
        y_s = tail(o_b, yc, ga, y_s.reshape(s_cnt, d),
                   p_sample[i].reshape(s_cnt, -1)).reshape(s_cnt, 1, d)

        outs[0].append(k_p.reshape(bsz, seq, nh, 2, HEAD_DIM))
        outs[1].append(v_p.reshape(bsz, seq, nh, 2 * HEAD_DIM))
        outs[2].append(c_p)
        outs[3].append(k_s.reshape(s_cnt, 1, nh, 2, HEAD_DIM))
        outs[4].append(v_s.reshape(s_cnt, 1, nh, 2 * HEAD_DIM))
        outs[5].append(c_s)
    return (y_p, y_s) + tuple(jnp.stack(o) for o in outs)
```
